```python
import math
import jax, jax.numpy as jnp
from jax import lax
import numpy as np

D_MODEL = 1024
BATCH = 1
SEQ = 16384
DEPTH = 1

D_MIX = D_MODEL
CONV_WIDTH = D_MIX // 2
CONV_K = 3
RWKV_HEAD = 64
RWKV_WIDTH = D_MIX - CONV_WIDTH
RWKV_HEADS = RWKV_WIDTH // RWKV_HEAD
DECAY_LORA = max(32, int(round(1.8 * math.sqrt(RWKV_WIDTH) / 32)) * 32)
AAA_LORA = max(32, int(round(1.8 * math.sqrt(RWKV_WIDTH) / 32)) * 32)
GATE_LORA = max(32, int(round(0.6 * RWKV_WIDTH ** 0.8 / 32)) * 32)
RWKV_COLS = 3 * RWKV_WIDTH + DECAY_LORA + AAA_LORA + GATE_LORA
IN_COLS = 3 * CONV_WIDTH + RWKV_COLS
D_FF = 128 * ((8 * D_MODEL // 3 + 127) // 128)
MACARON_W = 0.5
NORM_EPS = 1e-6
GN_EPS = 64e-5
N_ADA = 9

kernel_name = "hybrid_conv_rwkv7_macaron_adaln"


def rms_norm(x, g):
    xf = x.astype(jnp.float32)
    y = xf * lax.rsqrt(jnp.mean(xf * xf, axis=-1, keepdims=True) + NORM_EPS)
    return (y * g.astype(jnp.float32)).astype(x.dtype)


def modulate(x, g_pre, shift, scale):
    return rms_norm(x, g_pre) * (1 + scale[:, None, :]) + shift[:, None, :]


def swiglu(u, w1, w3, w2):
    return (jax.nn.silu(u @ w1) * (u @ w3)) @ w2


def time_shift(u):
    return jnp.pad(u, ((0, 0), (1, 0), (0, 0)))[:, :-1]


def causal_depthwise_conv(u, w):
    return lax.conv_general_dilated(
        u, w[:, None, :].astype(u.dtype), window_strides=(1,),
        padding=((CONV_K - 1, 0),), dimension_numbers=('NWC', 'WIO', 'NWC'),
        feature_group_count=u.shape[-1])


def rwkv7_scan(r, decay, k, v, kk, a):
    def step(S, inp):
        r_t, w_t, k_t, v_t, kk_t, a_t = inp
        sa = jnp.einsum('bhvk,bhk->bhv', S, -kk_t)
        S = (S * w_t[:, :, None, :]
             + sa[..., None] * (kk_t * a_t)[:, :, None, :]
             + v_t[..., None] * k_t[:, :, None, :])
        y = jnp.einsum('bhvk,bhk->bhv', S, r_t)
        return S, y
    Bsz = r.shape[0]
    xs = tuple(jnp.moveaxis(t.astype(jnp.float32), 1, 0) for t in (r, decay, k, v, kk, a))
    S0 = jnp.zeros((Bsz, RWKV_HEADS, RWKV_HEAD, RWKV_HEAD), jnp.float32)
    _, ys = lax.scan(step, S0, xs)
    return jnp.moveaxis(ys, 0, 1)


def mixer(h, w_in, conv_w, mu_shift, w0, w_up, a0, a_up, g_up, k_k, k_a, r_k,
          ln_x_w, ln_x_b, w_out):
    Bsz, T, _ = h.shape
    f = lambda t: t.astype(jnp.float32)
    p = h @ w_in
    C = CONV_WIDTH
    c_pre, c_post, c_val, rw = jnp.split(p, [C, 2 * C, 3 * C], axis=-1)
    y_conv = c_post * causal_depthwise_conv(c_pre * c_val, conv_w)
    rw = f(rw + (time_shift(rw) - rw) * mu_shift)
    R = RWKV_WIDTH
    xr, xk, xv, xw, xa, xg = jnp.split(
        rw, [R, 2 * R, 3 * R, 3 * R + DECAY_LORA, 3 * R + DECAY_LORA + AAA_LORA], axis=-1)
    w_raw = -jax.nn.softplus(-(f(w0) + jnp.tanh(xw) @ f(w_up))) - 0.5
    decay = jnp.exp(-jnp.exp(w_raw))
    a = jax.nn.sigmoid(f(a0) + xa @ f(a_up))
    g = jax.nn.sigmoid(xg) @ f(g_up)
    hs = lambda t: t.reshape(Bsz, T, RWKV_HEADS, RWKV_HEAD)
    hp = lambda t: f(t).reshape(RWKV_HEADS, RWKV_HEAD)
    r, k, v, decay, a = hs(xr), hs(xk), hs(xv), hs(decay), hs(a)
    kk = k * hp(k_k)
    kk = kk / jnp.maximum(jnp.sqrt(jnp.sum(kk * kk, axis=-1, keepdims=True)), 1e-12)
    k = k * (1 + (a - 1) * hp(k_a))
    o = rwkv7_scan(r, decay, k, v, kk, a)
    o_mean = jnp.mean(o, axis=-1, keepdims=True)
    o_var = jnp.mean(jnp.square(o - o_mean), axis=-1, keepdims=True)
    o = ((o - o_mean) * lax.rsqrt(o_var + GN_EPS)).reshape(Bsz, T, R) * f(ln_x_w) + f(ln_x_b)
    bonus = (jnp.sum(r * k * f(r_k), axis=-1, keepdims=True) * v).reshape(Bsz, T, R)
    y_rwkv = ((o + bonus) * g).astype(h.dtype)
    return jnp.concatenate([y_conv, y_rwkv], axis=-1) @ w_out


def setup_inputs(seed: int = 0) -> dict:
    key = jax.random.key(seed)
    keys = jax.random.split(key, 32)
    counter = [0]

    def nk():
        kk = keys[counter[0]]
        counter[0] += 1
        return kk

    nrm = lambda shape, s: jax.random.normal(nk(), shape, jnp.float32) * s
    uni = lambda shape, lo, hi: jax.random.uniform(nk(), shape, jnp.float32, lo, hi)
    L, D = DEPTH, D_MODEL
    return {
        "x": nrm((BATCH, SEQ, D), 1.0),
        "c": nrm((BATCH, D), 1.0),
        "w_ada": nrm((L, D, N_ADA * D), 0.5 * D ** -0.5),
        "b_ada": nrm((L, N_ADA * D), 0.02),
        "ffn1_g_pre": 1.0 + nrm((L, D), 0.02),
        "ffn1_w1": nrm((L, D, D_FF), D ** -0.5),
        "ffn1_w3": nrm((L, D, D_FF), D ** -0.5),
        "ffn1_w2": nrm((L, D_FF, D), D_FF ** -0.5),
        "ffn1_g_post": 1.0 + nrm((L, D), 0.02),
        "mix_g_pre": 1.0 + nrm((L, D), 0.02),
        "w_in": nrm((L, D, IN_COLS), D ** -0.5),
        "conv_w": nrm((L, CONV_K, CONV_WIDTH), CONV_K ** -0.5),
        "mu_shift": uni((L, RWKV_COLS), 0.0, 1.0),
        "w0": uni((L, RWKV_WIDTH), -5.0, 0.0),
        "w_up": nrm((L, DECAY_LORA, RWKV_WIDTH), 0.1 * DECAY_LORA ** -0.5),
        "a0": nrm((L, RWKV_WIDTH), 0.1),
        "a_up": nrm((L, AAA_LORA, RWKV_WIDTH), 0.1 * AAA_LORA ** -0.5),
        "g_up": nrm((L, GATE_LORA, RWKV_WIDTH), GATE_LORA ** -0.5),
        "k_k": 0.85 + nrm((L, RWKV_WIDTH), 0.02),
        "k_a": 1.0 + nrm((L, RWKV_WIDTH), 0.02),
        "r_k": nrm((L, RWKV_HEADS, RWKV_HEAD), 0.1),
        "ln_x_w": 1.0 + nrm((L, RWKV_WIDTH), 0.02),
        "ln_x_b": nrm((L, RWKV_WIDTH), 0.02),
        "w_out": nrm((L, D_MIX, D), D_MIX ** -0.5),
        "mix_g_post": 1.0 + nrm((L, D), 0.02),
        "ffn2_g_pre": 1.0 + nrm((L, D), 0.02),
        "ffn2_w1": nrm((L, D, D_FF), D ** -0.5),
        "ffn2_w3": nrm((L, D, D_FF), D ** -0.5),
        "ffn2_w2": nrm((L, D_FF, D), D_FF ** -0.5),
        "ffn2_g_post": 1.0 + nrm((L, D), 0.02),
    }


def reference(x, c, w_ada, b_ada,
              ffn1_g_pre, ffn1_w1, ffn1_w3, ffn1_w2, ffn1_g_post,
              mix_g_pre, w_in, conv_w, mu_shift, w0, w_up, a0, a_up, g_up,
              k_k, k_a, r_k, ln_x_w, ln_x_b, w_out, mix_g_post,
              ffn2_g_pre, ffn2_w1, ffn2_w3, ffn2_w2, ffn2_g_post):
    Bsz = x.shape[0]
    cond = jax.nn.silu(c)
    h = x
    for l in range(DEPTH):
        ada = (cond @ w_ada[l] + b_ada[l]).reshape(Bsz, N_ADA, D_MODEL)
        sh1, sc1, gt1, sh2, sc2, gt2, sh3, sc3, gt3 = [ada[:, i] for i in range(N_ADA)]
        u = modulate(h, ffn1_g_pre[l], sh1, sc1)
        y = swiglu(u, ffn1_w1[l], ffn1_w3[l], ffn1_w2[l])
        h = h + MACARON_W * gt1[:, None, :] * rms_norm(y, ffn1_g_post[l])
        u = modulate(h, mix_g_pre[l], sh2, sc2)
        y = mixer(u, w_in[l], conv_w[l], mu_shift[l], w0[l], w_up[l], a0[l], a_up[l],
                  g_up[l], k_k[l], k_a[l], r_k[l], ln_x_w[l], ln_x_b[l], w_out[l])
        h = h + gt2[:, None, :] * rms_norm(y, mix_g_post[l])
        u = modulate(h, ffn2_g_pre[l], sh3, sc3)
        y = swiglu(u, ffn2_w1[l], ffn2_w3[l], ffn2_w2[l])
        h = h + MACARON_W * gt3[:, None, :] * rms_norm(y, ffn2_g_post[l])
    return h
```

```python
import functools

import jax
import jax.numpy as jnp
from jax import lax
from jax.experimental import pallas as pl
from jax.experimental.pallas import tpu as pltpu

F32 = jnp.float32
BF16 = jnp.bfloat16

D_MODEL = 1024
D_FF = 2816
CONV_WIDTH = 512
CONV_K = 3
RWKV_WIDTH = 512
RWKV_HEAD = 64
RWKV_HEADS = RWKV_WIDTH // RWKV_HEAD
DECAY_LORA = 32
AAA_LORA = 32
GATE_LORA = 96
LORA_COLS = DECAY_LORA + AAA_LORA + GATE_LORA
N_ADA = 9
MACARON_W = 0.5
NORM_EPS = 1e-6
GN_EPS = 64e-5

LANES = 128
SUBLANES = 8
MXU_DIM = 256
LORA_PAD = MXU_DIM
HEADS_PER_PAIR = LANES // RWKV_HEAD
N_PAIRS = RWKV_WIDTH // LANES
CHUNK = 64
SUB = 16
MIX_TILE = 256
FFN_TILE = 512
FFN_COLS = MXU_DIM
ADA_COLS = 1024
VMEM_LIMIT = 56 * 1024 * 1024


def _rms(x):
    return x * lax.rsqrt(jnp.mean(x * x, axis=-1, keepdims=True) + NORM_EPS)


def _dot(a, b):
    return jnp.dot(a.astype(BF16), b.astype(BF16), preferred_element_type=F32)


def _dot_nt(a, b):
    return lax.dot_general(a.astype(BF16), b.astype(BF16), (((1,), (1,)), ((), ())),
                           preferred_element_type=F32)


def _dot_tn(a, b):
    return lax.dot_general(a.astype(BF16), b.astype(BF16), (((0,), (0,)), ((), ())),
                           preferred_element_type=F32)


def _ada_kernel(c_ref, w_ref, b_ref, o_ref):
    c = c_ref[...]
    cond = c * jax.nn.sigmoid(c)
    o_ref[0] = jnp.sum(w_ref[...] * cond, axis=0, keepdims=True) + b_ref[0]


def _ada(c, w_ada, b_ada):
    n_blk = (N_ADA * D_MODEL) // ADA_COLS
    return pl.pallas_call(
        _ada_kernel,
        grid=(n_blk,),
        in_specs=[
            pl.BlockSpec((D_MODEL, 1), lambda j: (0, 0)),
            pl.BlockSpec((D_MODEL, ADA_COLS), lambda j: (0, j)),
            pl.BlockSpec((1, 1, ADA_COLS), lambda j: (j, 0, 0)),
        ],
        out_specs=pl.BlockSpec((1, 1, ADA_COLS), lambda j: (j, 0, 0)),
        out_shape=jax.ShapeDtypeStruct((n_blk, 1, ADA_COLS), F32),
        compiler_params=pltpu.CompilerParams(dimension_semantics=("arbitrary",)),
        name="ada_proj",
    )(c.reshape(D_MODEL, 1), w_ada, b_ada.reshape(n_blk, 1, ADA_COLS))


def _ffn_kernel(ada_row, h_ref, ada_ref, gpre_ref, w1_ref, w3_ref, w2_ref, gpost_ref,
                o_ref, act_ref):
    x = h_ref[...]
    shift, scale, gate = ada_ref[ada_row], ada_ref[ada_row + 1], ada_ref[ada_row + 2]
    u = (_rms(x) * gpre_ref[...] * (1.0 + scale) + shift).astype(BF16)
    for c in range(D_FF // FFN_COLS):
        cols = slice(c * FFN_COLS, (c + 1) * FFN_COLS)
        a = jnp.dot(u, w1_ref[:, cols], preferred_element_type=F32)
        b = jnp.dot(u, w3_ref[:, cols], preferred_element_type=F32)
        act_ref[:, cols] = (a * jax.nn.sigmoid(a) * b).astype(BF16)
    y = jnp.dot(act_ref[...], w2_ref[...], preferred_element_type=F32)
    o_ref[...] = x + MACARON_W * gate * (_rms(y) * gpost_ref[...])


def _ffn(h, ada, ada_row, g_pre, w1, w3, w2, g_post):
    t = h.shape[0]
    tile = min(FFN_TILE, t)
    const = lambda i: (0, 0)
    return pl.pallas_call(
        functools.partial(_ffn_kernel, ada_row),
        grid=(t // tile,),
        in_specs=[
            pl.BlockSpec((tile, D_MODEL), lambda i: (i, 0)),
            pl.BlockSpec((N_ADA, 1, D_MODEL), lambda i: (0, 0, 0)),
            pl.BlockSpec((1, D_MODEL), const),
            pl.BlockSpec((D_MODEL, D_FF), const, pipeline_mode=pl.Buffered(1)),
            pl.BlockSpec((D_MODEL, D_FF), const, pipeline_mode=pl.Buffered(1)),
            pl.BlockSpec((D_FF, D_MODEL), const, pipeline_mode=pl.Buffered(1)),
            pl.BlockSpec((1, D_MODEL), const),
        ],
        out_specs=pl.BlockSpec((tile, D_MODEL), lambda i: (i, 0)),
        out_shape=jax.ShapeDtypeStruct((t, D_MODEL), F32),
        scratch_shapes=[pltpu.VMEM((tile, D_FF), BF16)],
        compiler_params=pltpu.CompilerParams(dimension_semantics=("arbitrary",),
                                             vmem_limit_bytes=VMEM_LIMIT),
        name="swiglu_half_step",
    )(h, ada, g_pre.reshape(1, D_MODEL), w1.astype(BF16), w3.astype(BF16), w2.astype(BF16),
      g_post.reshape(1, D_MODEL))


def _shift_rows(x, fill_row):
    row = lax.broadcasted_iota(jnp.int32, x.shape, 0)
    return jnp.where(row == 0, fill_row, pltpu.roll(x, 1, 0))


def _softplus(z):
    return jnp.maximum(z, 0.0) + jnp.log1p(jnp.exp(-jnp.abs(z)))


def _block_diag(x, left):
    return jnp.concatenate([jnp.where(left, x, 0.0), jnp.where(left, 0.0, x)], axis=0)


def _chunk_pair(ph, rh, qh, kh, pt, rt, qb, kb, v, masks):
    left, strict, incl, diag_blk, eye, bd_mask = masks
    bd = lambda x: _block_diag(x, left)
    mm = lambda a, b: _dot(a, bd(b))
    gram = _dot_nt(jnp.concatenate([ph, rh], axis=0),
                   jnp.concatenate([bd(qh), bd(kh)], axis=0))
    a_pq = jnp.where(strict, gram[:CHUNK, :LANES], 0.0)
    a_rq = jnp.where(incl, gram[CHUNK:, :LANES], 0.0)
    a_pk = jnp.where(strict, gram[:CHUNK, LANES:], 0.0)
    a_rk = jnp.where(incl, gram[CHUNK:, LANES:], 0.0)
    a_d = jnp.where(diag_blk, a_pq, 0.0)
    a_o = a_pq - a_d
    x2 = mm(a_d, a_d)
    x4 = mm(x2, x2)
    x8 = mm(x4, x4)
    t_d = eye + a_d
    t_d = t_d + mm(t_d, x2)
    t_d = t_d + mm(t_d, x4)
    t_d = t_d + mm(t_d, x8)
    n1 = mm(t_d, a_o)
    n2 = mm(n1, n1)
    s = eye + n1
    s = s + mm(s, n2)
    t_inv = mm(s, t_d)
    av = mm(a_pk, v)
    wu = _dot(t_inv, jnp.concatenate([bd(pt), bd(av)], axis=1))
    w, ub = wu[:, :LANES], wu[:, LANES:]
    rw = rt + mm(a_rq, w)
    y0 = _dot(jnp.concatenate([a_rq, a_rk], axis=1), jnp.concatenate([bd(ub), bd(v)], axis=0))
    g_off = jnp.where(bd_mask, _dot_tn(qb, w), 0.0)
    c_bd = jnp.where(bd_mask, _dot_tn(jnp.concatenate([qb, kb], axis=0),
                                      jnp.concatenate([ub, v], axis=0)), 0.0)
    return rw, y0, g_off, c_bd


def _mixer_kernel(h_ref, ada_ref, gpre_ref, win_ref, wlo_ref, mu_ref, mulo_ref, convw_ref,
                  w0_ref, a0_ref, loraup_ref, kk_ref, ka_ref, rk_ref, lnw_ref, lnb_ref,
                  ones_ref, wout_ref, gpost_ref, o_ref,
                  prev_rw_ref, prev_lo_ref, prev_cv_ref, state_ref):
    tile = h_ref.shape[0]
    n_chunks = tile // CHUNK
    last = SUBLANES - 1

    @pl.when(pl.program_id(0) == 0)
    def _():
        prev_rw_ref[...] = jnp.zeros_like(prev_rw_ref)
        prev_lo_ref[...] = jnp.zeros_like(prev_lo_ref)
        prev_cv_ref[...] = jnp.zeros_like(prev_cv_ref)
        state_ref[...] = jnp.zeros_like(state_ref)

    x = h_ref[...]
    shift, scale, gate = ada_ref[3], ada_ref[4], ada_ref[5]
    u = (_rms(x) * gpre_ref[...] * (1.0 + scale) + shift).astype(BF16)
    p = jnp.dot(u, win_ref[...], preferred_element_type=F32)
    plo = jnp.dot(u, wlo_ref[...], preferred_element_type=F32)

    cw = CONV_WIDTH
    c_post = p[:, cw:2 * cw]
    cv = p[:, :cw] * p[:, 2 * cw:3 * cw]
    cv1 = _shift_rows(cv, prev_cv_ref[last:last + 1, :])
    cv2 = _shift_rows(cv1, prev_cv_ref[last - 1:last, :])
    y_conv = c_post * (convw_ref[0:1, :] * cv2 + convw_ref[1:2, :] * cv1 + convw_ref[2:3, :] * cv)
    prev_cv_ref[...] = cv[tile - SUBLANES:, :]

    rw_raw = p[:, 3 * cw:]
    rw_mix = rw_raw + (_shift_rows(rw_raw, prev_rw_ref[last:last + 1, :]) - rw_raw) * mu_ref[...]
    lo_mix = plo + (_shift_rows(plo, prev_lo_ref[last:last + 1, :]) - plo) * mulo_ref[...]
    prev_rw_ref[...] = rw_raw[tile - SUBLANES:, :]
    prev_lo_ref[...] = plo[tile - SUBLANES:, :]
    rwd = RWKV_WIDTH
    xr, xk, xv = rw_mix[:, :rwd], rw_mix[:, rwd:2 * rwd], rw_mix[:, 2 * rwd:]
    lane_lo = lax.broadcasted_iota(jnp.int32, lo_mix.shape, 1)
    lo_act = jnp.where(lane_lo < DECAY_LORA, jnp.tanh(lo_mix),
                       jnp.where(lane_lo < DECAY_LORA + AAA_LORA, lo_mix, jax.nn.sigmoid(lo_mix)))
    lup = _dot(lo_act, loraup_ref[...])
    w_raw = -_softplus(-(w0_ref[...] + lup[:, :rwd])) - 0.5
    logw = -jnp.exp(w_raw)
    a = jax.nn.sigmoid(a0_ref[...] + lup[:, rwd:2 * rwd])
    g = lup[:, 2 * rwd:]
    ones_bd = ones_ref[...]
    kk = xk * kk_ref[...]
    kk = kk / jnp.maximum(jnp.sqrt(_dot(kk * kk, ones_bd)), 1e-12)
    k = xk * (1.0 + (a - 1.0) * ka_ref[...])
    bonus = _dot(xr * k * rk_ref[...], ones_bd) * xv
    pv = -kk
    qv = kk * a

    ti = lax.broadcasted_iota(jnp.int32, (tile, tile), 0)
    si = lax.broadcasted_iota(jnp.int32, (tile, tile), 1)
    same = (ti // CHUNK) == (si // CHUNK)
    tl, sl = ti % CHUNK, si % CHUNK
    mid = CHUNK // 2 - 1
    dmat = jnp.where(same, (sl <= tl).astype(F32) - (sl <= mid).astype(F32), 0.0)
    gc = jnp.dot(dmat, logw, preferred_element_type=F32, precision=lax.Precision.HIGHEST)
    e_fwd = jnp.exp(gc)
    e_bwd = jnp.exp(-gc)
    e_prev = jnp.exp(gc - logw)
    per_chunk = lambda z: z.reshape(n_chunks, CHUNK, rwd)
    gc3, lw3 = per_chunk(gc), per_chunk(logw)
    g_mid = lw3[:, 0:1, :] - gc3[:, 0:1, :]
    g_end = gc3[:, CHUNK - 1:CHUNK, :]
    e_mid = jnp.exp(g_mid)
    e_end = jnp.exp(g_end)
    decay_end = jnp.exp(g_mid + g_end)
    scale_chunk = lambda z, f: (per_chunk(z) * f).reshape(tile, rwd)
    ph = pv * e_prev
    rh = xr * e_fwd
    qh = qv * e_bwd
    kh = k * e_bwd
    pt = scale_chunk(ph, e_mid)
    rt = scale_chunk(rh, e_mid)
    qb = scale_chunk(qh, e_end)
    kb = scale_chunk(kh, e_end)

    r_i = lax.broadcasted_iota(jnp.int32, (CHUNK, LANES), 0)
    l_i = lax.broadcasted_iota(jnp.int32, (CHUNK, LANES), 1)
    col = l_i % RWKV_HEAD
    r2 = lax.broadcasted_iota(jnp.int32, (LANES, LANES), 0)
    c2 = lax.broadcasted_iota(jnp.int32, (LANES, LANES), 1)
    masks = (l_i < RWKV_HEAD, r_i > col, r_i >= col, (r_i // SUB) == (col // SUB),
             (r_i == col).astype(F32), (r2 // RWKV_HEAD) == (c2 // RWKV_HEAD))
    eye2 = r2 == c2

    o_rows = []
    for c in range(n_chunks):
        rows = slice(c * CHUNK, (c + 1) * CHUNK)
        o_pairs = []
        for j in range(N_PAIRS):
            lanes = slice(j * LANES, (j + 1) * LANES)
            sl_ = lambda z: z[rows, lanes]
            rw, y0, g_off, c_bd = _chunk_pair(sl_(ph), sl_(rh), sl_(qh), sl_(kh), sl_(pt), sl_(rt),
                                              sl_(qb), sl_(kb), sl_(xv), masks)
            m0 = state_ref[j]
            o_pairs.append(_dot(rw, m0) + y0)
            dec = jnp.where(eye2, decay_end[c, :, lanes], 0.0)
            state_ref[j] = jnp.sum(dec, axis=1, keepdims=True) * m0 + _dot(g_off, m0) + c_bd
        o_rows.append(jnp.concatenate(o_pairs, axis=1))
    o = jnp.concatenate(o_rows, axis=0)

    inv_n = 1.0 / RWKV_HEAD
    o_mean = _dot(o, ones_bd) * inv_n
    o_c = o - o_mean
    o_var = _dot(o_c * o_c, ones_bd) * inv_n
    o_n = o_c * lax.rsqrt(o_var + GN_EPS) * lnw_ref[...] + lnb_ref[...]
    y_rwkv = (o_n + bonus) * g
    y = _dot(jnp.concatenate([y_conv, y_rwkv], axis=1), wout_ref[...])
    o_ref[...] = x + gate * (_rms(y) * gpost_ref[...])


def _mixer(h, ada, g_pre, w_in, conv_w, mu_shift, w0, w_up, a0, a_up, g_up, k_k, k_a, r_k,
           ln_x_w, ln_x_b, w_out, g_post):
    t = h.shape[0]
    tile = min(MIX_TILE, t)
    n_main = 3 * CONV_WIDTH + 3 * RWKV_WIDTH
    w_main = w_in[:, :n_main].astype(BF16)
    w_lo = jnp.pad(w_in[:, n_main:], ((0, 0), (0, LORA_PAD - LORA_COLS))).astype(BF16)
    mu_main = mu_shift[:3 * RWKV_WIDTH].reshape(1, -1)
    mu_lo = jnp.pad(mu_shift[3 * RWKV_WIDTH:], (0, LORA_PAD - LORA_COLS)).reshape(1, -1)
    lora_up = jnp.zeros((LORA_PAD, 3 * RWKV_WIDTH), F32)
    lora_up = lora_up.at[:DECAY_LORA, :RWKV_WIDTH].set(w_up)
    lora_up = lora_up.at[DECAY_LORA:DECAY_LORA + AAA_LORA, RWKV_WIDTH:2 * RWKV_WIDTH].set(a_up)
    lora_up = lora_up.at[DECAY_LORA + AAA_LORA:LORA_COLS, 2 * RWKV_WIDTH:].set(g_up)
    head_id = jnp.arange(RWKV_WIDTH) // RWKV_HEAD
    ones_bd = (head_id[:, None] == head_id[None, :]).astype(BF16)
    row = lambda z: z.reshape(1, -1)
    const = lambda i: (0, 0)
    full = lambda shape: pl.BlockSpec(shape, const)
    return pl.pallas_call(
        _mixer_kernel,
        grid=(t // tile,),
        in_specs=[
            pl.BlockSpec((tile, D_MODEL), lambda i: (i, 0)),
            pl.BlockSpec((N_ADA, 1, D_MODEL), lambda i: (0, 0, 0)),
            full((1, D_MODEL)),
            full((D_MODEL, n_main)),
            full((D_MODEL, LORA_PAD)),
            full((1, 3 * RWKV_WIDTH)),
            full((1, LORA_PAD)),
            full((CONV_K, CONV_WIDTH)),
            full((1, RWKV_WIDTH)),
            full((1, RWKV_WIDTH)),
            full((LORA_PAD, 3 * RWKV_WIDTH)),
            full((1, RWKV_WIDTH)),
            full((1, RWKV_WIDTH)),
            full((1, RWKV_WIDTH)),
            full((1, RWKV_WIDTH)),
            full((1, RWKV_WIDTH)),
            full((RWKV_WIDTH, RWKV_WIDTH)),
            full((D_MODEL, D_MODEL)),
            full((1, D_MODEL)),
        ],
        out_specs=pl.BlockSpec((tile, D_MODEL), lambda i: (i, 0)),
        out_shape=jax.ShapeDtypeStruct((t, D_MODEL), F32),
        scratch_shapes=[
            pltpu.VMEM((SUBLANES, 3 * RWKV_WIDTH), F32),
            pltpu.VMEM((SUBLANES, LORA_PAD), F32),
            pltpu.VMEM((SUBLANES, CONV_WIDTH), F32),
            pltpu.VMEM((N_PAIRS, LANES, LANES), F32),
        ],
        compiler_params=pltpu.CompilerParams(dimension_semantics=("arbitrary",),
                                             vmem_limit_bytes=VMEM_LIMIT),
        name="token_mixing",
    )(h, ada, row(g_pre), w_main, w_lo, mu_main, mu_lo, conv_w, row(w0), row(a0),
      lora_up.astype(BF16), row(k_k), row(k_a), row(r_k), row(ln_x_w), row(ln_x_b), ones_bd,
      w_out.astype(BF16), row(g_post))


def kernel(x, c, w_ada, b_ada, ffn1_g_pre, ffn1_w1, ffn1_w3, ffn1_w2, ffn1_g_post, mix_g_pre, w_in, conv_w, mu_shift, w0, w_up, a0, a_up, g_up, k_k, k_a, r_k, ln_x_w, ln_x_b, w_out, mix_g_post, ffn2_g_pre, ffn2_w1, ffn2_w3, ffn2_w2, ffn2_g_post):
    bsz, t, _ = x.shape
    outs = []
    for b in range(bsz):
        h = x[b]
        for l in range(w_ada.shape[0]):
            ada = _ada(c[b], w_ada[l], b_ada[l])
            h = _ffn(h, ada, 0, ffn1_g_pre[l], ffn1_w1[l], ffn1_w3[l], ffn1_w2[l], ffn1_g_post[l])
            h = _mixer(h, ada, mix_g_pre[l], w_in[l], conv_w[l], mu_shift[l], w0[l], w_up[l],
                       a0[l], a_up[l], g_up[l], k_k[l], k_a[l], r_k[l], ln_x_w[l], ln_x_b[l],
                       w_out[l], mix_g_post[l])
            h = _ffn(h, ada, 6, ffn2_g_pre[l], ffn2_w1[l], ffn2_w3[l], ffn2_w2[l], ffn2_g_post[l])
        outs.append(h)
    return jnp.stack(outs, axis=0)
```

```python
import functools

import jax
import jax.numpy as jnp
from jax import lax
from jax.experimental import pallas as pl
from jax.experimental.pallas import tpu as pltpu

F32 = jnp.float32
BF16 = jnp.bfloat16

D_MODEL = 1024
D_FF = 2816
CONV_WIDTH = 512
CONV_K = 3
RWKV_WIDTH = 512
RWKV_HEAD = 64
RWKV_HEADS = RWKV_WIDTH // RWKV_HEAD
DECAY_LORA = 32
AAA_LORA = 32
GATE_LORA = 96
LORA_COLS = DECAY_LORA + AAA_LORA + GATE_LORA
N_ADA = 9
MACARON_W = 0.5
NORM_EPS = 1e-6
GN_EPS = 64e-5

LANES = 128
SUBLANES = 8
MXU_DIM = 256
LORA_PAD = MXU_DIM
HEADS_PER_PAIR = LANES // RWKV_HEAD
N_PAIRS = RWKV_WIDTH // LANES
CHUNK = 64
SUB = 16
MIX_TILE = 256
FFN_TILE = 512
FFN_COLS = MXU_DIM
ADA_COLS = 1024
VMEM_LIMIT = 56 * 1024 * 1024


def _rms(x):
    return x * lax.rsqrt(jnp.mean(x * x, axis=-1, keepdims=True) + NORM_EPS)


def _dot(a, b):
    return jnp.dot(a.astype(BF16), b.astype(BF16), preferred_element_type=F32)


def _dot_nt(a, b):
    return lax.dot_general(a.astype(BF16), b.astype(BF16), (((1,), (1,)), ((), ())),
                           preferred_element_type=F32)


def _dot_tn(a, b):
    return lax.dot_general(a.astype(BF16), b.astype(BF16), (((0,), (0,)), ((), ())),
                           preferred_element_type=F32)


def _ada_kernel(c_ref, w_ref, b_ref, o_ref):
    c = c_ref[...]
    cond = c * jax.nn.sigmoid(c)
    o_ref[0] = jnp.sum(w_ref[...] * cond, axis=0, keepdims=True) + b_ref[0]


def _ada(c, w_ada, b_ada):
    n_blk = (N_ADA * D_MODEL) // ADA_COLS
    return pl.pallas_call(
        _ada_kernel,
        grid=(n_blk,),
        in_specs=[
            pl.BlockSpec((D_MODEL, 1), lambda j: (0, 0)),
            pl.BlockSpec((D_MODEL, ADA_COLS), lambda j: (0, j)),
            pl.BlockSpec((1, 1, ADA_COLS), lambda j: (j, 0, 0)),
        ],
        out_specs=pl.BlockSpec((1, 1, ADA_COLS), lambda j: (j, 0, 0)),
        out_shape=jax.ShapeDtypeStruct((n_blk, 1, ADA_COLS), F32),
        compiler_params=pltpu.CompilerParams(dimension_semantics=("arbitrary",)),
        name="ada_proj",
    )(c.reshape(D_MODEL, 1), w_ada, b_ada.reshape(n_blk, 1, ADA_COLS))


def _ffn_kernel(ada_row, h_ref, ada_ref, gpre_ref, w1_ref, w3_ref, w2_ref, gpost_ref,
                o_ref, act_ref):
    x = h_ref[...]
    shift, scale, gate = ada_ref[ada_row], ada_ref[ada_row + 1], ada_ref[ada_row + 2]
    u = (_rms(x) * gpre_ref[...] * (1.0 + scale) + shift).astype(BF16)
    for c in range(D_FF // FFN_COLS):
        cols = slice(c * FFN_COLS, (c + 1) * FFN_COLS)
        a = jnp.dot(u, w1_ref[:, cols], preferred_element_type=F32)
        b = jnp.dot(u, w3_ref[:, cols], preferred_element_type=F32)
        act_ref[:, cols] = (a * jax.nn.sigmoid(a) * b).astype(BF16)
    y = jnp.dot(act_ref[...], w2_ref[...], preferred_element_type=F32)
    o_ref[...] = x + MACARON_W * gate * (_rms(y) * gpost_ref[...])


def _ffn(h, ada, ada_row, g_pre, w1, w3, w2, g_post):
    t = h.shape[0]
    tile = min(FFN_TILE, t)
    const = lambda i: (0, 0)
    return pl.pallas_call(
        functools.partial(_ffn_kernel, ada_row),
        grid=(t // tile,),
        in_specs=[
            pl.BlockSpec((tile, D_MODEL), lambda i: (i, 0)),
            pl.BlockSpec((N_ADA, 1, D_MODEL), lambda i: (0, 0, 0)),
            pl.BlockSpec((1, D_MODEL), const),
            pl.BlockSpec((D_MODEL, D_FF), const, pipeline_mode=pl.Buffered(1)),
            pl.BlockSpec((D_MODEL, D_FF), const, pipeline_mode=pl.Buffered(1)),
            pl.BlockSpec((D_FF, D_MODEL), const, pipeline_mode=pl.Buffered(1)),
            pl.BlockSpec((1, D_MODEL), const),
        ],
        out_specs=pl.BlockSpec((tile, D_MODEL), lambda i: (i, 0)),
        out_shape=jax.ShapeDtypeStruct((t, D_MODEL), F32),
        scratch_shapes=[pltpu.VMEM((tile, D_FF), BF16)],
        compiler_params=pltpu.CompilerParams(dimension_semantics=("arbitrary",),
                                             vmem_limit_bytes=VMEM_LIMIT),
        name="swiglu_half_step",
    )(h, ada, g_pre.reshape(1, D_MODEL), w1.astype(BF16), w3.astype(BF16), w2.astype(BF16),
      g_post.reshape(1, D_MODEL))


def _shift_rows(x, fill_row):
    row = lax.broadcasted_iota(jnp.int32, x.shape, 0)
    return jnp.where(row == 0, fill_row, pltpu.roll(x, 1, 0))


def _softplus(z):
    return jnp.maximum(z, 0.0) + jnp.log1p(jnp.exp(-jnp.abs(z)))


def _bmm(a, b):
    return jnp.einsum("bmk,bkn->bmn", a.astype(BF16), b.astype(BF16), preferred_element_type=F32)


def _bmm_nt(a, b):
    return jnp.einsum("bmk,bnk->bmn", a.astype(BF16), b.astype(BF16), preferred_element_type=F32)


def _bmm_tn(a, b):
    return jnp.einsum("bkm,bkn->bmn", a.astype(BF16), b.astype(BF16), preferred_element_type=F32)


def _block_diag(x, left):
    return jnp.concatenate([jnp.where(left, x, 0.0), jnp.where(left, 0.0, x)], axis=1)


def _chunk_terms(ph, rh, qh, kh, pt, rt, qb, kb, v, masks):
    left, strict, incl, diag_blk, eye, bd_mask = masks
    bd = lambda x: _block_diag(x, left)
    mm = lambda a, b: _bmm(a, bd(b))
    gram = _bmm_nt(jnp.concatenate([ph, rh], axis=1),
                   jnp.concatenate([bd(qh), bd(kh)], axis=1))
    a_pq = jnp.where(strict, gram[:, :CHUNK, :LANES], 0.0)
    a_rq = jnp.where(incl, gram[:, CHUNK:, :LANES], 0.0)
    a_pk = jnp.where(strict, gram[:, :CHUNK, LANES:], 0.0)
    a_rk = jnp.where(incl, gram[:, CHUNK:, LANES:], 0.0)
    a_d = jnp.where(diag_blk, a_pq, 0.0)
    a_o = a_pq - a_d
    x2 = mm(a_d, a_d)
    x4 = mm(x2, x2)
    x8 = mm(x4, x4)
    t_d = eye + a_d
    t_d = t_d + mm(t_d, x2)
    t_d = t_d + mm(t_d, x4)
    t_d = t_d + mm(t_d, x8)
    n1 = mm(t_d, a_o)
    n2 = mm(n1, n1)
    s = eye + n1
    s = s + mm(s, n2)
    t_inv = mm(s, t_d)
    av = mm(a_pk, v)
    wu = _bmm(t_inv, jnp.concatenate([bd(pt), bd(av)], axis=2))
    w, ub = wu[:, :, :LANES], wu[:, :, LANES:]
    rw = rt + mm(a_rq, w)
    y0 = _bmm(jnp.concatenate([a_rq, a_rk], axis=2), jnp.concatenate([bd(ub), bd(v)], axis=1))
    g_off = jnp.where(bd_mask, _bmm_tn(qb, w), 0.0)
    c_bd = jnp.where(bd_mask, _bmm_tn(jnp.concatenate([qb, kb], axis=1),
                                      jnp.concatenate([ub, v], axis=1)), 0.0)
    return rw, y0, g_off, c_bd


def _mixer_kernel(h_ref, ada_ref, gpre_ref, win_ref, wlo_ref, mu_ref, mulo_ref, convw_ref,
                  w0_ref, a0_ref, loraup_ref, kk_ref, ka_ref, rk_ref, lnw_ref, lnb_ref,
                  ones_ref, wout_ref, gpost_ref, o_ref,
                  prev_rw_ref, prev_lo_ref, prev_cv_ref, state_ref):
    tile = h_ref.shape[0]
    n_chunks = tile // CHUNK
    last = SUBLANES - 1

    @pl.when(pl.program_id(0) == 0)
    def _():
        prev_rw_ref[...] = jnp.zeros_like(prev_rw_ref)
        prev_lo_ref[...] = jnp.zeros_like(prev_lo_ref)
        prev_cv_ref[...] = jnp.zeros_like(prev_cv_ref)
        state_ref[...] = jnp.zeros_like(state_ref)

    x = h_ref[...]
    shift, scale, gate = ada_ref[3], ada_ref[4], ada_ref[5]
    u = (_rms(x) * gpre_ref[...] * (1.0 + scale) + shift).astype(BF16)
    p = jnp.dot(u, win_ref[...], preferred_element_type=F32)
    plo = jnp.dot(u, wlo_ref[...], preferred_element_type=F32)

    cw = CONV_WIDTH
    c_post = p[:, cw:2 * cw]
    cv = p[:, :cw] * p[:, 2 * cw:3 * cw]
    cv1 = _shift_rows(cv, prev_cv_ref[last:last + 1, :])
    cv2 = _shift_rows(cv1, prev_cv_ref[last - 1:last, :])
    y_conv = c_post * (convw_ref[0:1, :] * cv2 + convw_ref[1:2, :] * cv1 + convw_ref[2:3, :] * cv)
    prev_cv_ref[...] = cv[tile - SUBLANES:, :]

    rw_raw = p[:, 3 * cw:]
    rw_mix = rw_raw + (_shift_rows(rw_raw, prev_rw_ref[last:last + 1, :]) - rw_raw) * mu_ref[...]
    lo_mix = plo + (_shift_rows(plo, prev_lo_ref[last:last + 1, :]) - plo) * mulo_ref[...]
    prev_rw_ref[...] = rw_raw[tile - SUBLANES:, :]
    prev_lo_ref[...] = plo[tile - SUBLANES:, :]
    rwd = RWKV_WIDTH
    xr, xk, xv = rw_mix[:, :rwd], rw_mix[:, rwd:2 * rwd], rw_mix[:, 2 * rwd:]
    lane_lo = lax.broadcasted_iota(jnp.int32, lo_mix.shape, 1)
    lo_act = jnp.where(lane_lo < DECAY_LORA, jnp.tanh(lo_mix),
                       jnp.where(lane_lo < DECAY_LORA + AAA_LORA, lo_mix, jax.nn.sigmoid(lo_mix)))
    lup = _dot(lo_act, loraup_ref[...])
    w_raw = -_softplus(-(w0_ref[...] + lup[:, :rwd])) - 0.5
    logw = -jnp.exp(w_raw)
    a = jax.nn.sigmoid(a0_ref[...] + lup[:, rwd:2 * rwd])
    g = lup[:, 2 * rwd:]
    ones_bd = ones_ref[...]
    kk = xk * kk_ref[...]
    kk = kk / jnp.maximum(jnp.sqrt(_dot(kk * kk, ones_bd)), 1e-12)
    k = xk * (1.0 + (a - 1.0) * ka_ref[...])
    bonus = _dot(xr * k * rk_ref[...], ones_bd) * xv
    pv = -kk
    qv = kk * a

    ti = lax.broadcasted_iota(jnp.int32, (tile, tile), 0)
    si = lax.broadcasted_iota(jnp.int32, (tile, tile), 1)
    same = (ti // CHUNK) == (si // CHUNK)
    tl, sl = ti % CHUNK, si % CHUNK
    mid = CHUNK // 2 - 1
    dmat = jnp.where(same, (sl <= tl).astype(F32) - (sl <= mid).astype(F32), 0.0)
    gc = jnp.dot(dmat, logw, preferred_element_type=F32, precision=lax.Precision.HIGHEST)
    e_fwd = jnp.exp(gc)
    e_bwd = jnp.exp(-gc)
    e_prev = jnp.exp(gc - logw)
    per_chunk = lambda z: z.reshape(n_chunks, CHUNK, rwd)
    gc3, lw3 = per_chunk(gc), per_chunk(logw)
    g_mid = lw3[:, 0:1, :] - gc3[:, 0:1, :]
    g_end = gc3[:, CHUNK - 1:CHUNK, :]
    e_mid = jnp.exp(g_mid)
    e_end = jnp.exp(g_end)
    decay_end = jnp.exp(g_mid + g_end)
    scale_chunk = lambda z, f: (per_chunk(z) * f).reshape(tile, rwd)
    ph = pv * e_prev
    rh = xr * e_fwd
    qh = qv * e_bwd
    kh = k * e_bwd
    pt = scale_chunk(ph, e_mid)
    rt = scale_chunk(rh, e_mid)
    qb = scale_chunk(qh, e_end)
    kb = scale_chunk(kh, e_end)

    r_i = lax.broadcasted_iota(jnp.int32, (CHUNK, LANES), 0)
    l_i = lax.broadcasted_iota(jnp.int32, (CHUNK, LANES), 1)
    col = l_i % RWKV_HEAD
    r2 = lax.broadcasted_iota(jnp.int32, (LANES, LANES), 0)
    c2 = lax.broadcasted_iota(jnp.int32, (LANES, LANES), 1)
    masks = (l_i < RWKV_HEAD, r_i > col, r_i >= col, (r_i // SUB) == (col // SUB),
             (r_i == col).astype(F32), (r2 // RWKV_HEAD) == (c2 // RWKV_HEAD))
    eye2 = r2 == c2

    def batch(z):
        return jnp.stack([z[c * CHUNK:(c + 1) * CHUNK, j * LANES:(j + 1) * LANES]
                          for c in range(n_chunks) for j in range(N_PAIRS)], axis=0)

    rw, y0, g_off, c_bd = _chunk_terms(batch(ph), batch(rh), batch(qh), batch(kh), batch(pt),
                                       batch(rt), batch(qb), batch(kb), batch(xv), masks)
    m = state_ref[...]
    o_rows = []
    for c in range(n_chunks):
        b = slice(c * N_PAIRS, (c + 1) * N_PAIRS)
        o_c = _bmm(rw[b], m) + y0[b]
        o_rows.append(jnp.concatenate([o_c[j] for j in range(N_PAIRS)], axis=1))
        dec = jnp.stack([jnp.where(eye2, decay_end[c, :, j * LANES:(j + 1) * LANES], 0.0)
                         for j in range(N_PAIRS)], axis=0)
        m = jnp.sum(dec, axis=2, keepdims=True) * m + _bmm(g_off[b], m) + c_bd[b]
    state_ref[...] = m
    o = jnp.concatenate(o_rows, axis=0)

    inv_n = 1.0 / RWKV_HEAD
    o_mean = _dot(o, ones_bd) * inv_n
    o_c = o - o_mean
    o_var = _dot(o_c * o_c, ones_bd) * inv_n
    o_n = o_c * lax.rsqrt(o_var + GN_EPS) * lnw_ref[...] + lnb_ref[...]
    y_rwkv = (o_n + bonus) * g
    y = _dot(jnp.concatenate([y_conv, y_rwkv], axis=1), wout_ref[...])
    o_ref[...] = x + gate * (_rms(y) * gpost_ref[...])


def _mixer(h, ada, g_pre, w_in, conv_w, mu_shift, w0, w_up, a0, a_up, g_up, k_k, k_a, r_k,
           ln_x_w, ln_x_b, w_out, g_post):
    t = h.shape[0]
    tile = min(MIX_TILE, t)
    n_main = 3 * CONV_WIDTH + 3 * RWKV_WIDTH
    w_main = w_in[:, :n_main].astype(BF16)
    w_lo = jnp.pad(w_in[:, n_main:], ((0, 0), (0, LORA_PAD - LORA_COLS))).astype(BF16)
    mu_main = mu_shift[:3 * RWKV_WIDTH].reshape(1, -1)
    mu_lo = jnp.pad(mu_shift[3 * RWKV_WIDTH:], (0, LORA_PAD - LORA_COLS)).reshape(1, -1)
    lora_up = jnp.zeros((LORA_PAD, 3 * RWKV_WIDTH), F32)
    lora_up = lora_up.at[:DECAY_LORA, :RWKV_WIDTH].set(w_up)
    lora_up = lora_up.at[DECAY_LORA:DECAY_LORA + AAA_LORA, RWKV_WIDTH:2 * RWKV_WIDTH].set(a_up)
    lora_up = lora_up.at[DECAY_LORA + AAA_LORA:LORA_COLS, 2 * RWKV_WIDTH:].set(g_up)
    head_id = jnp.arange(RWKV_WIDTH) // RWKV_HEAD
    ones_bd = (head_id[:, None] == head_id[None, :]).astype(BF16)
    row = lambda z: z.reshape(1, -1)
    const = lambda i: (0, 0)
    full = lambda shape: pl.BlockSpec(shape, const)
    return pl.pallas_call(
        _mixer_kernel,
        grid=(t // tile,),
        in_specs=[
            pl.BlockSpec((tile, D_MODEL), lambda i: (i, 0)),
            pl.BlockSpec((N_ADA, 1, D_MODEL), lambda i: (0, 0, 0)),
            full((1, D_MODEL)),
            full((D_MODEL, n_main)),
            full((D_MODEL, LORA_PAD)),
            full((1, 3 * RWKV_WIDTH)),
            full((1, LORA_PAD)),
            full((CONV_K, CONV_WIDTH)),
            full((1, RWKV_WIDTH)),
            full((1, RWKV_WIDTH)),
            full((LORA_PAD, 3 * RWKV_WIDTH)),
            full((1, RWKV_WIDTH)),
            full((1, RWKV_WIDTH)),
            full((1, RWKV_WIDTH)),
            full((1, RWKV_WIDTH)),
            full((1, RWKV_WIDTH)),
            full((RWKV_WIDTH, RWKV_WIDTH)),
            full((D_MODEL, D_MODEL)),
            full((1, D_MODEL)),
        ],
        out_specs=pl.BlockSpec((tile, D_MODEL), lambda i: (i, 0)),
        out_shape=jax.ShapeDtypeStruct((t, D_MODEL), F32),
        scratch_shapes=[
            pltpu.VMEM((SUBLANES, 3 * RWKV_WIDTH), F32),
            pltpu.VMEM((SUBLANES, LORA_PAD), F32),
            pltpu.VMEM((SUBLANES, CONV_WIDTH), F32),
            pltpu.VMEM((N_PAIRS, LANES, LANES), F32),
        ],
        compiler_params=pltpu.CompilerParams(dimension_semantics=("arbitrary",),
                                             vmem_limit_bytes=VMEM_LIMIT),
        name="token_mixing",
    )(h, ada, row(g_pre), w_main, w_lo, mu_main, mu_lo, conv_w, row(w0), row(a0),
      lora_up.astype(BF16), row(k_k), row(k_a), row(r_k), row(ln_x_w), row(ln_x_b), ones_bd,
      w_out.astype(BF16), row(g_post))


def kernel(x, c, w_ada, b_ada, ffn1_g_pre, ffn1_w1, ffn1_w3, ffn1_w2, ffn1_g_post, mix_g_pre, w_in, conv_w, mu_shift, w0, w_up, a0, a_up, g_up, k_k, k_a, r_k, ln_x_w, ln_x_b, w_out, mix_g_post, ffn2_g_pre, ffn2_w1, ffn2_w3, ffn2_w2, ffn2_g_post):
    bsz, t, _ = x.shape
    outs = []
    for b in range(bsz):
        h = x[b]
        for l in range(w_ada.shape[0]):
            ada = _ada(c[b], w_ada[l], b_ada[l])
            h = _ffn(h, ada, 0, ffn1_g_pre[l], ffn1_w1[l], ffn1_w3[l], ffn1_w2[l], ffn1_g_post[l])
            h = _mixer(h, ada, mix_g_pre[l], w_in[l], conv_w[l], mu_shift[l], w0[l], w_up[l],
                       a0[l], a_up[l], g_up[l], k_k[l], k_a[l], r_k[l], ln_x_w[l], ln_x_b[l],
                       w_out[l], mix_g_post[l])
            h = _ffn(h, ada, 6, ffn2_g_pre[l], ffn2_w1[l], ffn2_w3[l], ffn2_w2[l], ffn2_g_post[l])
        outs.append(h)
    return jnp.stack(outs, axis=0)
```

```python
import functools

import jax
import jax.numpy as jnp
from jax import lax
from jax.experimental import pallas as pl
from jax.experimental.pallas import tpu as pltpu

F32 = jnp.float32
BF16 = jnp.bfloat16

D_MODEL = 1024
D_FF = 2816
CONV_WIDTH = 512
CONV_K = 3
RWKV_WIDTH = 512
RWKV_HEAD = 64
RWKV_HEADS = RWKV_WIDTH // RWKV_HEAD
DECAY_LORA = 32
AAA_LORA = 32
GATE_LORA = 96
LORA_COLS = DECAY_LORA + AAA_LORA + GATE_LORA
N_ADA = 9
MACARON_W = 0.5
NORM_EPS = 1e-6
GN_EPS = 64e-5

LANES = 128
SUBLANES = 8
MXU_DIM = 256
LORA_PAD = MXU_DIM
HEADS_PER_PAIR = LANES // RWKV_HEAD
N_PAIRS = RWKV_WIDTH // LANES
CHUNK = 64
SUB = 16
MIX_TILE = 512
FFN_TILE = 512
FFN_COLS = MXU_DIM
ADA_COLS = 1024
VMEM_LIMIT = 56 * 1024 * 1024


def _rms(x):
    return x * lax.rsqrt(jnp.mean(x * x, axis=-1, keepdims=True) + NORM_EPS)


def _dot(a, b):
    return jnp.dot(a.astype(BF16), b.astype(BF16), preferred_element_type=F32)


def _dot_nt(a, b):
    return lax.dot_general(a.astype(BF16), b.astype(BF16), (((1,), (1,)), ((), ())),
                           preferred_element_type=F32)


def _dot_tn(a, b):
    return lax.dot_general(a.astype(BF16), b.astype(BF16), (((0,), (0,)), ((), ())),
                           preferred_element_type=F32)


def _ada_kernel(c_ref, w_ref, b_ref, o_ref):
    c = c_ref[...]
    cond = c * jax.nn.sigmoid(c)
    o_ref[0] = jnp.sum(w_ref[...] * cond, axis=0, keepdims=True) + b_ref[0]


def _ada(c, w_ada, b_ada):
    n_blk = (N_ADA * D_MODEL) // ADA_COLS
    return pl.pallas_call(
        _ada_kernel,
        grid=(n_blk,),
        in_specs=[
            pl.BlockSpec((D_MODEL, 1), lambda j: (0, 0)),
            pl.BlockSpec((D_MODEL, ADA_COLS), lambda j: (0, j)),
            pl.BlockSpec((1, 1, ADA_COLS), lambda j: (j, 0, 0)),
        ],
        out_specs=pl.BlockSpec((1, 1, ADA_COLS), lambda j: (j, 0, 0)),
        out_shape=jax.ShapeDtypeStruct((n_blk, 1, ADA_COLS), F32),
        compiler_params=pltpu.CompilerParams(dimension_semantics=("arbitrary",)),
        name="ada_proj",
    )(c.reshape(D_MODEL, 1), w_ada, b_ada.reshape(n_blk, 1, ADA_COLS))


def _ffn_kernel(ada_row, h_ref, ada_ref, gpre_ref, w1_ref, w3_ref, w2_ref, gpost_ref,
                o_ref, act_ref):
    x = h_ref[...]
    shift, scale, gate = ada_ref[ada_row], ada_ref[ada_row + 1], ada_ref[ada_row + 2]
    u = (_rms(x) * gpre_ref[...] * (1.0 + scale) + shift).astype(BF16)
    for c in range(D_FF // FFN_COLS):
        cols = slice(c * FFN_COLS, (c + 1) * FFN_COLS)
        a = jnp.dot(u, w1_ref[:, cols], preferred_element_type=F32)
        b = jnp.dot(u, w3_ref[:, cols], preferred_element_type=F32)
        act_ref[:, cols] = (a * jax.nn.sigmoid(a) * b).astype(BF16)
    y = jnp.dot(act_ref[...], w2_ref[...], preferred_element_type=F32)
    o_ref[...] = x + MACARON_W * gate * (_rms(y) * gpost_ref[...])


def _ffn(h, ada, ada_row, g_pre, w1, w3, w2, g_post):
    t = h.shape[0]
    tile = min(FFN_TILE, t)
    const = lambda i: (0, 0)
    return pl.pallas_call(
        functools.partial(_ffn_kernel, ada_row),
        grid=(t // tile,),
        in_specs=[
            pl.BlockSpec((tile, D_MODEL), lambda i: (i, 0)),
            pl.BlockSpec((N_ADA, 1, D_MODEL), lambda i: (0, 0, 0)),
            pl.BlockSpec((1, D_MODEL), const),
            pl.BlockSpec((D_MODEL, D_FF), const, pipeline_mode=pl.Buffered(1)),
            pl.BlockSpec((D_MODEL, D_FF), const, pipeline_mode=pl.Buffered(1)),
            pl.BlockSpec((D_FF, D_MODEL), const, pipeline_mode=pl.Buffered(1)),
            pl.BlockSpec((1, D_MODEL), const),
        ],
        out_specs=pl.BlockSpec((tile, D_MODEL), lambda i: (i, 0)),
        out_shape=jax.ShapeDtypeStruct((t, D_MODEL), F32),
        scratch_shapes=[pltpu.VMEM((tile, D_FF), BF16)],
        compiler_params=pltpu.CompilerParams(dimension_semantics=("arbitrary",),
                                             vmem_limit_bytes=VMEM_LIMIT),
        name="swiglu_half_step",
    )(h, ada, g_pre.reshape(1, D_MODEL), w1.astype(BF16), w3.astype(BF16), w2.astype(BF16),
      g_post.reshape(1, D_MODEL))


def _shift_rows(x, fill_row):
    row = lax.broadcasted_iota(jnp.int32, x.shape, 0)
    return jnp.where(row == 0, fill_row, pltpu.roll(x, 1, 0))


def _softplus(z):
    return jnp.maximum(z, 0.0) + jnp.log1p(jnp.exp(-jnp.abs(z)))


def _bmm(a, b):
    return jnp.einsum("bmk,bkn->bmn", a.astype(BF16), b.astype(BF16), preferred_element_type=F32)


def _bmm_nt(a, b):
    return jnp.einsum("bmk,bnk->bmn", a.astype(BF16), b.astype(BF16), preferred_element_type=F32)


def _bmm_tn(a, b):
    return jnp.einsum("bkm,bkn->bmn", a.astype(BF16), b.astype(BF16), preferred_element_type=F32)


def _block_diag(x, left):
    return jnp.concatenate([jnp.where(left, x, 0.0), jnp.where(left, 0.0, x)], axis=1)


def _chunk_terms(ph, rh, qh, kh, pt, rt, qb, kb, v, masks):
    left, strict, incl, diag_blk, eye, bd_mask = masks
    bd = lambda x: _block_diag(x, left)
    mm = lambda a, b: _bmm(a, bd(b))
    gram = _bmm_nt(jnp.concatenate([ph, rh], axis=1),
                   jnp.concatenate([bd(qh), bd(kh)], axis=1))
    a_pq = jnp.where(strict, gram[:, :CHUNK, :LANES], 0.0)
    a_rq = jnp.where(incl, gram[:, CHUNK:, :LANES], 0.0)
    a_pk = jnp.where(strict, gram[:, :CHUNK, LANES:], 0.0)
    a_rk = jnp.where(incl, gram[:, CHUNK:, LANES:], 0.0)
    a_d = jnp.where(diag_blk, a_pq, 0.0)
    a_o = a_pq - a_d
    x2 = mm(a_d, a_d)
    t_d = eye + a_d
    both = mm(jnp.concatenate([x2, t_d], axis=1), x2)
    x4, t_d = both[:, :CHUNK], t_d + both[:, CHUNK:]
    both = mm(jnp.concatenate([x4, t_d], axis=1), x4)
    x8, t_d = both[:, :CHUNK], t_d + both[:, CHUNK:]
    t_d = t_d + mm(t_d, x8)
    n1 = mm(t_d, a_o)
    n2 = mm(n1, n1)
    s = eye + n1
    s = s + mm(s, n2)
    t_inv = mm(s, t_d)
    both = mm(jnp.concatenate([a_pk, a_rk], axis=1), v)
    av, ark_v = both[:, :CHUNK], both[:, CHUNK:]
    wu = _bmm(t_inv, jnp.concatenate([bd(pt), bd(av)], axis=2))
    w, ub = wu[:, :, :LANES], wu[:, :, LANES:]
    ru = _bmm(a_rq, jnp.concatenate([bd(w), bd(ub)], axis=2))
    rw = rt + ru[:, :, :LANES]
    y0 = ru[:, :, LANES:] + ark_v
    g_off = jnp.where(bd_mask, _bmm_tn(qb, w), 0.0)
    c_bd = jnp.where(bd_mask, _bmm_tn(jnp.concatenate([qb, kb], axis=1),
                                      jnp.concatenate([ub, v], axis=1)), 0.0)
    return rw, y0, g_off, c_bd


def _mixer_kernel(h_ref, ada_ref, gpre_ref, win_ref, wlo_ref, mu_ref, mulo_ref, convw_ref,
                  w0_ref, a0_ref, loraup_ref, kk_ref, ka_ref, rk_ref, lnw_ref, lnb_ref,
                  ones_ref, wout_ref, gpost_ref, o_ref,
                  prev_rw_ref, prev_lo_ref, prev_cv_ref, state_ref):
    tile = h_ref.shape[0]
    n_chunks = tile // CHUNK
    last = SUBLANES - 1

    @pl.when(pl.program_id(0) == 0)
    def _():
        prev_rw_ref[...] = jnp.zeros_like(prev_rw_ref)
        prev_lo_ref[...] = jnp.zeros_like(prev_lo_ref)
        prev_cv_ref[...] = jnp.zeros_like(prev_cv_ref)
        state_ref[...] = jnp.zeros_like(state_ref)

    x = h_ref[...]
    shift, scale, gate = ada_ref[3], ada_ref[4], ada_ref[5]
    u = (_rms(x) * gpre_ref[...] * (1.0 + scale) + shift).astype(BF16)
    p = jnp.dot(u, win_ref[...], preferred_element_type=F32)
    plo = jnp.dot(u, wlo_ref[...], preferred_element_type=F32)

    cw = CONV_WIDTH
    c_post = p[:, cw:2 * cw]
    cv = p[:, :cw] * p[:, 2 * cw:3 * cw]
    cv1 = _shift_rows(cv, prev_cv_ref[last:last + 1, :])
    cv2 = _shift_rows(cv1, prev_cv_ref[last - 1:last, :])
    y_conv = c_post * (convw_ref[0:1, :] * cv2 + convw_ref[1:2, :] * cv1 + convw_ref[2:3, :] * cv)
    prev_cv_ref[...] = cv[tile - SUBLANES:, :]

    rw_raw = p[:, 3 * cw:]
    rw_mix = rw_raw + (_shift_rows(rw_raw, prev_rw_ref[last:last + 1, :]) - rw_raw) * mu_ref[...]
    lo_mix = plo + (_shift_rows(plo, prev_lo_ref[last:last + 1, :]) - plo) * mulo_ref[...]
    prev_rw_ref[...] = rw_raw[tile - SUBLANES:, :]
    prev_lo_ref[...] = plo[tile - SUBLANES:, :]
    rwd = RWKV_WIDTH
    xr, xk, xv = rw_mix[:, :rwd], rw_mix[:, rwd:2 * rwd], rw_mix[:, 2 * rwd:]
    lane_lo = lax.broadcasted_iota(jnp.int32, lo_mix.shape, 1)
    lo_act = jnp.where(lane_lo < DECAY_LORA, jnp.tanh(lo_mix),
                       jnp.where(lane_lo < DECAY_LORA + AAA_LORA, lo_mix, jax.nn.sigmoid(lo_mix)))
    lup = _dot(lo_act, loraup_ref[...])
    w_raw = -_softplus(-(w0_ref[...] + lup[:, :rwd])) - 0.5
    logw = -jnp.exp(w_raw)
    a = jax.nn.sigmoid(a0_ref[...] + lup[:, rwd:2 * rwd])
    g = lup[:, 2 * rwd:]
    ones_bd = ones_ref[...]
    kk = xk * kk_ref[...]
    kk = kk / jnp.maximum(jnp.sqrt(_dot(kk * kk, ones_bd)), 1e-12)
    k = xk * (1.0 + (a - 1.0) * ka_ref[...])
    bonus = _dot(xr * k * rk_ref[...], ones_bd) * xv
    pv = -kk
    qv = kk * a

    ti = lax.broadcasted_iota(jnp.int32, (tile, tile), 0)
    si = lax.broadcasted_iota(jnp.int32, (tile, tile), 1)
    same = (ti // CHUNK) == (si // CHUNK)
    tl, sl = ti % CHUNK, si % CHUNK
    mid = CHUNK // 2 - 1
    dmat = jnp.where(same, (sl <= tl).astype(F32) - (sl <= mid).astype(F32), 0.0).astype(BF16)
    logw_hi = logw.astype(BF16)
    logw_lo = (logw - logw_hi.astype(F32)).astype(BF16)
    gc = (jnp.dot(dmat, logw_hi, preferred_element_type=F32)
          + jnp.dot(dmat, logw_lo, preferred_element_type=F32))
    e_fwd = jnp.exp(gc)
    e_bwd = jnp.exp(-gc)
    e_prev = jnp.exp(gc - logw)
    per_chunk = lambda z: z.reshape(n_chunks, CHUNK, rwd)
    gc3, lw3 = per_chunk(gc), per_chunk(logw)
    g_mid = lw3[:, 0:1, :] - gc3[:, 0:1, :]
    g_end = gc3[:, CHUNK - 1:CHUNK, :]
    e_mid = jnp.exp(g_mid)
    e_end = jnp.exp(g_end)
    decay_end = jnp.exp(g_mid + g_end)
    scale_chunk = lambda z, f: (per_chunk(z) * f).reshape(tile, rwd)
    ph = pv * e_prev
    rh = xr * e_fwd
    qh = qv * e_bwd
    kh = k * e_bwd
    pt = scale_chunk(ph, e_mid)
    rt = scale_chunk(rh, e_mid)
    qb = scale_chunk(qh, e_end)
    kb = scale_chunk(kh, e_end)

    r_i = lax.broadcasted_iota(jnp.int32, (CHUNK, LANES), 0)
    l_i = lax.broadcasted_iota(jnp.int32, (CHUNK, LANES), 1)
    col = l_i % RWKV_HEAD
    r2 = lax.broadcasted_iota(jnp.int32, (LANES, LANES), 0)
    c2 = lax.broadcasted_iota(jnp.int32, (LANES, LANES), 1)
    masks = (l_i < RWKV_HEAD, r_i > col, r_i >= col, (r_i // SUB) == (col // SUB),
             (r_i == col).astype(F32), (r2 // RWKV_HEAD) == (c2 // RWKV_HEAD))
    eye2 = r2 == c2

    def batch(z):
        return jnp.stack([z[c * CHUNK:(c + 1) * CHUNK, j * LANES:(j + 1) * LANES]
                          for c in range(n_chunks) for j in range(N_PAIRS)], axis=0)

    rw, y0, g_off, c_bd = _chunk_terms(batch(ph), batch(rh), batch(qh), batch(kh), batch(pt),
                                       batch(rt), batch(qb), batch(kb), batch(xv), masks)
    m = state_ref[...]
    o_rows = []
    for c in range(n_chunks):
        b = slice(c * N_PAIRS, (c + 1) * N_PAIRS)
        o_c = _bmm(rw[b], m) + y0[b]
        o_rows.append(jnp.concatenate([o_c[j] for j in range(N_PAIRS)], axis=1))
        dec = jnp.stack([jnp.where(eye2, decay_end[c, :, j * LANES:(j + 1) * LANES], 0.0)
                         for j in range(N_PAIRS)], axis=0)
        m = jnp.sum(dec, axis=2, keepdims=True) * m + _bmm(g_off[b], m) + c_bd[b]
    state_ref[...] = m
    o = jnp.concatenate(o_rows, axis=0)

    inv_n = 1.0 / RWKV_HEAD
    o_mean = _dot(o, ones_bd) * inv_n
    o_c = o - o_mean
    o_var = _dot(o_c * o_c, ones_bd) * inv_n
    o_n = o_c * lax.rsqrt(o_var + GN_EPS) * lnw_ref[...] + lnb_ref[...]
    y_rwkv = (o_n + bonus) * g
    y = _dot(jnp.concatenate([y_conv, y_rwkv], axis=1), wout_ref[...])
    o_ref[...] = x + gate * (_rms(y) * gpost_ref[...])


def _mixer(h, ada, g_pre, w_in, conv_w, mu_shift, w0, w_up, a0, a_up, g_up, k_k, k_a, r_k,
           ln_x_w, ln_x_b, w_out, g_post):
    t = h.shape[0]
    tile = min(MIX_TILE, t)
    n_main = 3 * CONV_WIDTH + 3 * RWKV_WIDTH
    w_main = w_in[:, :n_main].astype(BF16)
    w_lo = jnp.pad(w_in[:, n_main:], ((0, 0), (0, LORA_PAD - LORA_COLS))).astype(BF16)
    mu_main = mu_shift[:3 * RWKV_WIDTH].reshape(1, -1)
    mu_lo = jnp.pad(mu_shift[3 * RWKV_WIDTH:], (0, LORA_PAD - LORA_COLS)).reshape(1, -1)
    lora_up = jnp.zeros((LORA_PAD, 3 * RWKV_WIDTH), F32)
    lora_up = lora_up.at[:DECAY_LORA, :RWKV_WIDTH].set(w_up)
    lora_up = lora_up.at[DECAY_LORA:DECAY_LORA + AAA_LORA, RWKV_WIDTH:2 * RWKV_WIDTH].set(a_up)
    lora_up = lora_up.at[DECAY_LORA + AAA_LORA:LORA_COLS, 2 * RWKV_WIDTH:].set(g_up)
    head_id = jnp.arange(RWKV_WIDTH) // RWKV_HEAD
    ones_bd = (head_id[:, None] == head_id[None, :]).astype(BF16)
    row = lambda z: z.reshape(1, -1)
    const = lambda i: (0, 0)
    full = lambda shape: pl.BlockSpec(shape, const)
    return pl.pallas_call(
        _mixer_kernel,
        grid=(t // tile,),
        in_specs=[
            pl.BlockSpec((tile, D_MODEL), lambda i: (i, 0)),
            pl.BlockSpec((N_ADA, 1, D_MODEL), lambda i: (0, 0, 0)),
            full((1, D_MODEL)),
            full((D_MODEL, n_main)),
            full((D_MODEL, LORA_PAD)),
            full((1, 3 * RWKV_WIDTH)),
            full((1, LORA_PAD)),
            full((CONV_K, CONV_WIDTH)),
            full((1, RWKV_WIDTH)),
            full((1, RWKV_WIDTH)),
            full((LORA_PAD, 3 * RWKV_WIDTH)),
            full((1, RWKV_WIDTH)),
            full((1, RWKV_WIDTH)),
            full((1, RWKV_WIDTH)),
            full((1, RWKV_WIDTH)),
            full((1, RWKV_WIDTH)),
            full((RWKV_WIDTH, RWKV_WIDTH)),
            full((D_MODEL, D_MODEL)),
            full((1, D_MODEL)),
        ],
        out_specs=pl.BlockSpec((tile, D_MODEL), lambda i: (i, 0)),
        out_shape=jax.ShapeDtypeStruct((t, D_MODEL), F32),
        scratch_shapes=[
            pltpu.VMEM((SUBLANES, 3 * RWKV_WIDTH), F32),
            pltpu.VMEM((SUBLANES, LORA_PAD), F32),
            pltpu.VMEM((SUBLANES, CONV_WIDTH), F32),
            pltpu.VMEM((N_PAIRS, LANES, LANES), F32),
        ],
        compiler_params=pltpu.CompilerParams(dimension_semantics=("arbitrary",),
                                             vmem_limit_bytes=VMEM_LIMIT),
        name="token_mixing",
    )(h, ada, row(g_pre), w_main, w_lo, mu_main, mu_lo, conv_w, row(w0), row(a0),
      lora_up.astype(BF16), row(k_k), row(k_a), row(r_k), row(ln_x_w), row(ln_x_b), ones_bd,
      w_out.astype(BF16), row(g_post))


def kernel(x, c, w_ada, b_ada, ffn1_g_pre, ffn1_w1, ffn1_w3, ffn1_w2, ffn1_g_post, mix_g_pre, w_in, conv_w, mu_shift, w0, w_up, a0, a_up, g_up, k_k, k_a, r_k, ln_x_w, ln_x_b, w_out, mix_g_post, ffn2_g_pre, ffn2_w1, ffn2_w3, ffn2_w2, ffn2_g_post):
    bsz, t, _ = x.shape
    outs = []
    for b in range(bsz):
        h = x[b]
        for l in range(w_ada.shape[0]):
            ada = _ada(c[b], w_ada[l], b_ada[l])
            h = _ffn(h, ada, 0, ffn1_g_pre[l], ffn1_w1[l], ffn1_w3[l], ffn1_w2[l], ffn1_g_post[l])
            h = _mixer(h, ada, mix_g_pre[l], w_in[l], conv_w[l], mu_shift[l], w0[l], w_up[l],
                       a0[l], a_up[l], g_up[l], k_k[l], k_a[l], r_k[l], ln_x_w[l], ln_x_b[l],
                       w_out[l], mix_g_post[l])
            h = _ffn(h, ada, 6, ffn2_g_pre[l], ffn2_w1[l], ffn2_w3[l], ffn2_w2[l], ffn2_g_post[l])
        outs.append(h)
    return jnp.stack(outs, axis=0)
```

```python
import functools

import jax
import jax.numpy as jnp
from jax import lax
from jax.experimental import pallas as pl
from jax.experimental.pallas import tpu as pltpu

F32 = jnp.float32
BF16 = jnp.bfloat16

D_MODEL = 1024
D_FF = 2816
CONV_WIDTH = 512
CONV_K = 3
RWKV_WIDTH = 512
RWKV_HEAD = 64
RWKV_HEADS = RWKV_WIDTH // RWKV_HEAD
DECAY_LORA = 32
AAA_LORA = 32
GATE_LORA = 96
LORA_COLS = DECAY_LORA + AAA_LORA + GATE_LORA
N_ADA = 9
MACARON_W = 0.5
NORM_EPS = 1e-6
GN_EPS = 64e-5

LANES = 128
SUBLANES = 8
MXU_DIM = 256
LORA_PAD = MXU_DIM
HEADS_PER_PAIR = LANES // RWKV_HEAD
N_PAIRS = RWKV_WIDTH // LANES
CHUNK = 64
SUB = 16
MIX_TILE = 1024
MIX_SUBTILE = 256
MIX_PHASES = 4
MIX_STEPS = 8
FFN_TILE = 1024
FFN_SUBTILE = 256
FFN_LAG = 4
FFN_COLS = MXU_DIM
ADA_COLS = 1024
VMEM_LIMIT = 56 * 1024 * 1024


def _rms(x):
    return x * lax.rsqrt(jnp.mean(x * x, axis=-1, keepdims=True) + NORM_EPS)


def _dot(a, b):
    return jnp.dot(a.astype(BF16), b.astype(BF16), preferred_element_type=F32)


def _ada_kernel(c_ref, w_ref, b_ref, o_ref):
    c = c_ref[...]
    cond = c * jax.nn.sigmoid(c)
    o_ref[0] = jnp.sum(w_ref[...] * cond, axis=0, keepdims=True) + b_ref[0]


def _ada(c, w_ada, b_ada):
    n_blk = (N_ADA * D_MODEL) // ADA_COLS
    return pl.pallas_call(
        _ada_kernel,
        grid=(n_blk,),
        in_specs=[
            pl.BlockSpec((D_MODEL, 1), lambda j: (0, 0)),
            pl.BlockSpec((D_MODEL, ADA_COLS), lambda j: (0, j)),
            pl.BlockSpec((1, 1, ADA_COLS), lambda j: (j, 0, 0)),
        ],
        out_specs=pl.BlockSpec((1, 1, ADA_COLS), lambda j: (j, 0, 0)),
        out_shape=jax.ShapeDtypeStruct((n_blk, 1, ADA_COLS), F32),
        compiler_params=pltpu.CompilerParams(dimension_semantics=("arbitrary",)),
        name="ada_proj",
    )(c.reshape(D_MODEL, 1), w_ada, b_ada.reshape(n_blk, 1, ADA_COLS))


def _ffn_rows_steps(h_ref, o_ref, act_ref, row0, rows, c):
    x = h_ref[row0:row0 + rows, :]
    u = (_rms(x) * c["g_pre"] * (1.0 + c["scale"]) + c["shift"]).astype(BF16)
    yield
    for j in range(D_FF // FFN_COLS):
        cols = slice(j * FFN_COLS, (j + 1) * FFN_COLS)
        a = jnp.dot(u, c["w1"][:, cols], preferred_element_type=F32)
        b = jnp.dot(u, c["w3"][:, cols], preferred_element_type=F32)
        act_ref[row0:row0 + rows, cols] = (a * jax.nn.sigmoid(a) * b).astype(BF16)
        yield
    act = act_ref[row0:row0 + rows, :]
    y_cols = []
    for j in range(D_MODEL // FFN_COLS):
        cols = slice(j * FFN_COLS, (j + 1) * FFN_COLS)
        y_cols.append(jnp.dot(act, c["w2"][:, cols], preferred_element_type=F32))
        yield
    y = jnp.concatenate(y_cols, axis=1)
    o_ref[row0:row0 + rows, :] = x + MACARON_W * c["gate"] * (_rms(y) * c["g_post"])
    yield


def _ffn_kernel(ada_row, h_ref, ada_ref, gpre_ref, w1_ref, w3_ref, w2_ref, gpost_ref,
                o_ref, act_ref):
    tile = h_ref.shape[0]
    sub = min(FFN_SUBTILE, tile)
    consts = dict(shift=ada_ref[ada_row], scale=ada_ref[ada_row + 1], gate=ada_ref[ada_row + 2],
                  g_pre=gpre_ref[...], g_post=gpost_ref[...], w1=w1_ref, w3=w3_ref, w2=w2_ref)
    blocks = [_ffn_rows_steps(h_ref, o_ref, act_ref, r0, sub, consts) for r0 in range(0, tile, sub)]
    n_steps = 2 + D_FF // FFN_COLS + D_MODEL // FFN_COLS
    for step in range((len(blocks) - 1) * FFN_LAG + n_steps):
        for k, blk in enumerate(blocks):
            if 0 <= step - k * FFN_LAG < n_steps:
                next(blk)


def _ffn(h, ada, ada_row, g_pre, w1, w3, w2, g_post):
    t = h.shape[0]
    tile = min(FFN_TILE, t)
    const = lambda i: (0, 0)
    return pl.pallas_call(
        functools.partial(_ffn_kernel, ada_row),
        grid=(t // tile,),
        in_specs=[
            pl.BlockSpec((tile, D_MODEL), lambda i: (i, 0)),
            pl.BlockSpec((N_ADA, 1, D_MODEL), lambda i: (0, 0, 0)),
            pl.BlockSpec((1, D_MODEL), const),
            pl.BlockSpec((D_MODEL, D_FF), const, pipeline_mode=pl.Buffered(1)),
            pl.BlockSpec((D_MODEL, D_FF), const, pipeline_mode=pl.Buffered(1)),
            pl.BlockSpec((D_FF, D_MODEL), const, pipeline_mode=pl.Buffered(1)),
            pl.BlockSpec((1, D_MODEL), const),
        ],
        out_specs=pl.BlockSpec((tile, D_MODEL), lambda i: (i, 0)),
        out_shape=jax.ShapeDtypeStruct((t, D_MODEL), F32),
        scratch_shapes=[pltpu.VMEM((tile, D_FF), BF16)],
        compiler_params=pltpu.CompilerParams(dimension_semantics=("arbitrary",),
                                             vmem_limit_bytes=VMEM_LIMIT),
        name="swiglu_half_step",
    )(h, ada, g_pre.reshape(1, D_MODEL), w1.astype(BF16), w3.astype(BF16), w2.astype(BF16),
      g_post.reshape(1, D_MODEL))


def _shift_rows(x, fill_row):
    row = lax.broadcasted_iota(jnp.int32, x.shape, 0)
    return jnp.where(row == 0, fill_row, pltpu.roll(x, 1, 0))


def _softplus(z):
    return jnp.maximum(z, 0.0) + jnp.log1p(jnp.exp(-jnp.abs(z)))


def _bmm(a, b):
    return jnp.einsum("bmk,bkn->bmn", a.astype(BF16), b.astype(BF16), preferred_element_type=F32)


def _bmm_nt(a, b):
    return jnp.einsum("bmk,bnk->bmn", a.astype(BF16), b.astype(BF16), preferred_element_type=F32)


def _bmm_tn(a, b):
    return jnp.einsum("bkm,bkn->bmn", a.astype(BF16), b.astype(BF16), preferred_element_type=F32)


def _block_diag(x, left):
    return jnp.concatenate([jnp.where(left, x, 0.0), jnp.where(left, 0.0, x)], axis=1)


def _chunk_terms_steps(ph, rh, qh, kh, pt, rt, qb, kb, v, masks, res):
    left, strict, incl, diag_blk, eye, bd_mask = masks
    bd = lambda x: _block_diag(x, left)
    mm = lambda a, b: _bmm(a, bd(b))
    gram = _bmm_nt(jnp.concatenate([ph, rh], axis=1),
                   jnp.concatenate([bd(qh), bd(kh)], axis=1))
    a_pq = jnp.where(strict, gram[:, :CHUNK, :LANES], 0.0)
    a_rq = jnp.where(incl, gram[:, CHUNK:, :LANES], 0.0)
    a_pk = jnp.where(strict, gram[:, :CHUNK, LANES:], 0.0)
    a_rk = jnp.where(incl, gram[:, CHUNK:, LANES:], 0.0)
    yield
    a_d = jnp.where(diag_blk, a_pq, 0.0)
    a_o = a_pq - a_d
    x2 = mm(a_d, a_d)
    t_d = eye + a_d
    yield
    both = mm(jnp.concatenate([x2, t_d], axis=1), x2)
    x4, t_d = both[:, :CHUNK], t_d + both[:, CHUNK:]
    yield
    both = mm(jnp.concatenate([x4, t_d], axis=1), x4)
    x8, t_d = both[:, :CHUNK], t_d + both[:, CHUNK:]
    yield
    t_d = t_d + mm(t_d, x8)
    n1 = mm(t_d, a_o)
    yield
    n2 = mm(n1, n1)
    s = eye + n1
    s = s + mm(s, n2)
    yield
    t_inv = mm(s, t_d)
    both = mm(jnp.concatenate([a_pk, a_rk], axis=1), v)
    av, ark_v = both[:, :CHUNK], both[:, CHUNK:]
    yield
    wu = _bmm(t_inv, jnp.concatenate([bd(pt), bd(av)], axis=2))
    w, ub = wu[:, :, :LANES], wu[:, :, LANES:]
    ru = _bmm(a_rq, jnp.concatenate([bd(w), bd(ub)], axis=2))
    res["rw"] = rt + ru[:, :, :LANES]
    res["y0"] = ru[:, :, LANES:] + ark_v
    res["g_off"] = jnp.where(bd_mask, _bmm_tn(qb, w), 0.0)
    res["c_bd"] = jnp.where(bd_mask, _bmm_tn(jnp.concatenate([qb, kb], axis=1),
                                             jnp.concatenate([ub, v], axis=1)), 0.0)
    yield


def _seg_sum(z, ones_bd):
    half = ones_bd.shape[0]
    return jnp.concatenate([_dot(z[:, :half], ones_bd), _dot(z[:, half:], ones_bd)], axis=1)


def _mix_rows_steps(k, h_ref, o_ref, row0, rows, shared, c):
    n_chunks = rows // CHUNK
    cw, rwd = CONV_WIDTH, RWKV_WIDTH
    x = h_ref[row0:row0 + rows, :]
    u = (_rms(x) * c["g_pre"] * (1.0 + c["scale"]) + c["shift"]).astype(BF16)
    yield
    p_cols = []
    for j in range(6):
        p_cols.append(jnp.dot(u, c["w_in"][:, j * cw:(j + 1) * cw], preferred_element_type=F32))
        yield
    plo = jnp.dot(u, c["w_lo"][...], preferred_element_type=F32)
    c_pre, c_post, c_val = p_cols[0], p_cols[1], p_cols[2]
    rw_raw = jnp.concatenate(p_cols[3:], axis=1)
    cv = c_pre * c_val
    shared[k] = dict(prev_rw=rw_raw[rows - 1:, :], prev_lo=plo[rows - 1:, :],
                     prev_cv1=cv[rows - 1:, :], prev_cv2=cv[rows - 2:rows - 1, :])
    yield

    prev = shared[k - 1]
    conv_w = c["conv_w"]
    cv1 = _shift_rows(cv, prev["prev_cv1"])
    cv2 = _shift_rows(cv1, prev["prev_cv2"])
    y_conv = c_post * (conv_w[0:1, :] * cv2 + conv_w[1:2, :] * cv1 + conv_w[2:3, :] * cv)
    yield
    rw_mix = rw_raw + (_shift_rows(rw_raw, prev["prev_rw"]) - rw_raw) * c["mu"]
    lo_mix = plo + (_shift_rows(plo, prev["prev_lo"]) - plo) * c["mu_lo"]
    xr, xk, xv = rw_mix[:, :rwd], rw_mix[:, rwd:2 * rwd], rw_mix[:, 2 * rwd:]
    yield
    lane_lo = lax.broadcasted_iota(jnp.int32, lo_mix.shape, 1)
    lo_act = jnp.where(lane_lo < DECAY_LORA, jnp.tanh(lo_mix),
                       jnp.where(lane_lo < DECAY_LORA + AAA_LORA, lo_mix, jax.nn.sigmoid(lo_mix)))
    lup = _dot(lo_act, c["lora_up"][...])
    yield
    w_raw = -_softplus(-(c["w0"] + lup[:, :rwd])) - 0.5
    logw = -jnp.exp(w_raw)
    a = jax.nn.sigmoid(c["a0"] + lup[:, rwd:2 * rwd])
    g = lup[:, 2 * rwd:]
    yield
    ones_bd = c["ones"][...]
    kk = xk * c["k_k"]
    kk = kk / jnp.maximum(jnp.sqrt(_seg_sum(kk * kk, ones_bd)), 1e-12)
    kmod = xk * (1.0 + (a - 1.0) * c["k_a"])
    bonus = _seg_sum(xr * kmod * c["r_k"], ones_bd) * xv
    pv = -kk
    qv = kk * a
    yield
    logw_hi = logw.astype(BF16)
    logw_lo = (logw - logw_hi.astype(F32)).astype(BF16)
    gc = (jnp.dot(c["dmat"], logw_hi, preferred_element_type=F32)
          + jnp.dot(c["dmat"], logw_lo, preferred_element_type=F32))
    e_fwd = jnp.exp(gc)
    e_bwd = jnp.exp(-gc)
    e_prev = jnp.exp(gc - logw)
    yield
    per_chunk = lambda z: z.reshape(n_chunks, CHUNK, rwd)
    gc3, lw3 = per_chunk(gc), per_chunk(logw)
    g_mid = lw3[:, 0:1, :] - gc3[:, 0:1, :]
    g_end = gc3[:, CHUNK - 1:CHUNK, :]
    e_mid = jnp.exp(g_mid)
    e_end = jnp.exp(g_end)
    decay_end = jnp.exp(g_mid + g_end)
    scale_chunk = lambda z, f: (per_chunk(z) * f).reshape(rows, rwd)
    ph = pv * e_prev
    rh = xr * e_fwd
    qh = qv * e_bwd
    kh = kmod * e_bwd
    yield
    pt = scale_chunk(ph, e_mid)
    rt = scale_chunk(rh, e_mid)
    qb = scale_chunk(qh, e_end)
    kb = scale_chunk(kh, e_end)

    def batch(z):
        return jnp.stack([z[ci * CHUNK:(ci + 1) * CHUNK, j * LANES:(j + 1) * LANES]
                          for ci in range(n_chunks) for j in range(N_PAIRS)], axis=0)

    terms = {}
    chunk_steps = _chunk_terms_steps(batch(ph), batch(rh), batch(qh), batch(kh), batch(pt),
                                     batch(rt), batch(qb), batch(kb), batch(xv), c["masks"], terms)
    yield

    for _ in chunk_steps:
        yield

    m = shared[k - 1]["state"]
    o_rows = []
    for ci in range(n_chunks):
        b = slice(ci * N_PAIRS, (ci + 1) * N_PAIRS)
        o_c = _bmm(terms["rw"][b], m) + terms["y0"][b]
        o_rows.append(jnp.concatenate([o_c[j] for j in range(N_PAIRS)], axis=1))
        dec = jnp.stack([jnp.where(c["eye2"], decay_end[ci, :, j * LANES:(j + 1) * LANES], 0.0)
                         for j in range(N_PAIRS)], axis=0)
        m = jnp.sum(dec, axis=2, keepdims=True) * m + _bmm(terms["g_off"][b], m) + terms["c_bd"][b]
        if ci == n_chunks - 1:
            shared[k]["state"] = m
        yield
    o = jnp.concatenate(o_rows, axis=0)
    inv_n = 1.0 / RWKV_HEAD
    o_c = o - _seg_sum(o, ones_bd) * inv_n
    yield
    o_var = _seg_sum(o_c * o_c, ones_bd) * inv_n
    o_n = o_c * lax.rsqrt(o_var + GN_EPS) * c["ln_w"] + c["ln_b"]
    y_rwkv = (o_n + bonus) * g
    yield
    y = _dot(jnp.concatenate([y_conv, y_rwkv], axis=1), c["w_out"][...])
    yield
    o_ref[row0:row0 + rows, :] = x + c["gate"] * (_rms(y) * c["g_post"])
    yield


def _mixer_kernel(h_ref, ada_ref, gpre_ref, win_ref, wlo_ref, mu_ref, mulo_ref, convw_ref,
                  w0_ref, a0_ref, loraup_ref, kk_ref, ka_ref, rk_ref, lnw_ref, lnb_ref,
                  ones_ref, wout_ref, gpost_ref, o_ref,
                  prev_rw_ref, prev_lo_ref, prev_cv_ref, state_ref):
    tile = h_ref.shape[0]
    sub = min(MIX_SUBTILE, tile)
    assert sub // CHUNK == MIX_STEPS // 2

    @pl.when(pl.program_id(0) == 0)
    def _():
        prev_rw_ref[...] = jnp.zeros_like(prev_rw_ref)
        prev_lo_ref[...] = jnp.zeros_like(prev_lo_ref)
        prev_cv_ref[...] = jnp.zeros_like(prev_cv_ref)
        state_ref[...] = jnp.zeros_like(state_ref)

    ti = lax.broadcasted_iota(jnp.int32, (sub, sub), 0)
    si = lax.broadcasted_iota(jnp.int32, (sub, sub), 1)
    tl, sl = ti % CHUNK, si % CHUNK
    mid = CHUNK // 2 - 1
    dmat = jnp.where((ti // CHUNK) == (si // CHUNK),
                     (sl <= tl).astype(F32) - (sl <= mid).astype(F32), 0.0).astype(BF16)
    r_i = lax.broadcasted_iota(jnp.int32, (CHUNK, LANES), 0)
    l_i = lax.broadcasted_iota(jnp.int32, (CHUNK, LANES), 1)
    col = l_i % RWKV_HEAD
    r2 = lax.broadcasted_iota(jnp.int32, (LANES, LANES), 0)
    c2 = lax.broadcasted_iota(jnp.int32, (LANES, LANES), 1)
    masks = (l_i < RWKV_HEAD, r_i > col, r_i >= col, (r_i // SUB) == (col // SUB),
             (r_i == col).astype(F32), (r2 // RWKV_HEAD) == (c2 // RWKV_HEAD))
    consts = dict(
        shift=ada_ref[3], scale=ada_ref[4], gate=ada_ref[5], g_pre=gpre_ref[...],
        w_in=win_ref, w_lo=wlo_ref, mu=mu_ref[...], mu_lo=mulo_ref[...], conv_w=convw_ref[...],
        w0=w0_ref[...], a0=a0_ref[...], lora_up=loraup_ref, k_k=kk_ref[...], k_a=ka_ref[...],
        r_k=rk_ref[...], ln_w=lnw_ref[...], ln_b=lnb_ref[...], ones=ones_ref, w_out=wout_ref,
        g_post=gpost_ref[...], dmat=dmat, masks=masks, eye2=r2 == c2)

    n_sub = tile // sub
    shared = {-1: dict(prev_rw=prev_rw_ref[...], prev_lo=prev_lo_ref[...],
                       prev_cv1=prev_cv_ref[0:1, :], prev_cv2=prev_cv_ref[1:2, :],
                       state=state_ref[...])}
    blocks = [_mix_rows_steps(k, h_ref, o_ref, k * sub, sub, shared, consts) for k in range(n_sub)]
    for step in range((n_sub - 1 + MIX_PHASES) * MIX_STEPS):
        for k, blk in enumerate(blocks):
            if 0 <= step - k * MIX_STEPS < MIX_PHASES * MIX_STEPS:
                next(blk)
    final = shared[n_sub - 1]
    prev_rw_ref[...] = final["prev_rw"]
    prev_lo_ref[...] = final["prev_lo"]
    prev_cv_ref[0:1, :] = final["prev_cv1"]
    prev_cv_ref[1:2, :] = final["prev_cv2"]
    state_ref[...] = final["state"]


def _mixer(h, ada, g_pre, w_in, conv_w, mu_shift, w0, w_up, a0, a_up, g_up, k_k, k_a, r_k,
           ln_x_w, ln_x_b, w_out, g_post):
    t = h.shape[0]
    tile = min(MIX_TILE, t)
    n_main = 3 * CONV_WIDTH + 3 * RWKV_WIDTH
    w_main = w_in[:, :n_main].astype(BF16)
    w_lo = jnp.pad(w_in[:, n_main:], ((0, 0), (0, LORA_PAD - LORA_COLS))).astype(BF16)
    mu_main = mu_shift[:3 * RWKV_WIDTH].reshape(1, -1)
    mu_lo = jnp.pad(mu_shift[3 * RWKV_WIDTH:], (0, LORA_PAD - LORA_COLS)).reshape(1, -1)
    lora_up = jnp.zeros((LORA_PAD, 3 * RWKV_WIDTH), F32)
    lora_up = lora_up.at[:DECAY_LORA, :RWKV_WIDTH].set(w_up)
    lora_up = lora_up.at[DECAY_LORA:DECAY_LORA + AAA_LORA, RWKV_WIDTH:2 * RWKV_WIDTH].set(a_up)
    lora_up = lora_up.at[DECAY_LORA + AAA_LORA:LORA_COLS, 2 * RWKV_WIDTH:].set(g_up)
    head_id = jnp.arange(MXU_DIM) // RWKV_HEAD
    ones_bd = (head_id[:, None] == head_id[None, :]).astype(BF16)
    row = lambda z: z.reshape(1, -1)
    const = lambda i: (0, 0)
    full = lambda shape: pl.BlockSpec(shape, const)
    return pl.pallas_call(
        _mixer_kernel,
        grid=(t // tile,),
        in_specs=[
            pl.BlockSpec((tile, D_MODEL), lambda i: (i, 0)),
            pl.BlockSpec((N_ADA, 1, D_MODEL), lambda i: (0, 0, 0)),
            full((1, D_MODEL)),
            full((D_MODEL, n_main)),
            full((D_MODEL, LORA_PAD)),
            full((1, 3 * RWKV_WIDTH)),
            full((1, LORA_PAD)),
            full((CONV_K, CONV_WIDTH)),
            full((1, RWKV_WIDTH)),
            full((1, RWKV_WIDTH)),
            full((LORA_PAD, 3 * RWKV_WIDTH)),
            full((1, RWKV_WIDTH)),
            full((1, RWKV_WIDTH)),
            full((1, RWKV_WIDTH)),
            full((1, RWKV_WIDTH)),
            full((1, RWKV_WIDTH)),
            full((MXU_DIM, MXU_DIM)),
            full((D_MODEL, D_MODEL)),
            full((1, D_MODEL)),
        ],
        out_specs=pl.BlockSpec((tile, D_MODEL), lambda i: (i, 0)),
        out_shape=jax.ShapeDtypeStruct((t, D_MODEL), F32),
        scratch_shapes=[
            pltpu.VMEM((1, 3 * RWKV_WIDTH), F32),
            pltpu.VMEM((1, LORA_PAD), F32),
            pltpu.VMEM((2, CONV_WIDTH), F32),
            pltpu.VMEM((N_PAIRS, LANES, LANES), F32),
        ],
        compiler_params=pltpu.CompilerParams(dimension_semantics=("arbitrary",),
                                             vmem_limit_bytes=VMEM_LIMIT),
        name="token_mixing",
    )(h, ada, row(g_pre), w_main, w_lo, mu_main, mu_lo, conv_w, row(w0), row(a0),
      lora_up.astype(BF16), row(k_k), row(k_a), row(r_k), row(ln_x_w), row(ln_x_b), ones_bd,
      w_out.astype(BF16), row(g_post))


def kernel(x, c, w_ada, b_ada, ffn1_g_pre, ffn1_w1, ffn1_w3, ffn1_w2, ffn1_g_post, mix_g_pre, w_in, conv_w, mu_shift, w0, w_up, a0, a_up, g_up, k_k, k_a, r_k, ln_x_w, ln_x_b, w_out, mix_g_post, ffn2_g_pre, ffn2_w1, ffn2_w3, ffn2_w2, ffn2_g_post):
    bsz, t, _ = x.shape
    outs = []
    for b in range(bsz):
        h = x[b]
        for l in range(w_ada.shape[0]):
            ada = _ada(c[b], w_ada[l], b_ada[l])
            h = _ffn(h, ada, 0, ffn1_g_pre[l], ffn1_w1[l], ffn1_w3[l], ffn1_w2[l], ffn1_g_post[l])
            h = _mixer(h, ada, mix_g_pre[l], w_in[l], conv_w[l], mu_shift[l], w0[l], w_up[l],
                       a0[l], a_up[l], g_up[l], k_k[l], k_a[l], r_k[l], ln_x_w[l], ln_x_b[l],
                       w_out[l], mix_g_post[l])
            h = _ffn(h, ada, 6, ffn2_g_pre[l], ffn2_w1[l], ffn2_w3[l], ffn2_w2[l], ffn2_g_post[l])
        outs.append(h)
    return jnp.stack(outs, axis=0)
```

```python
import functools

import jax
import jax.numpy as jnp
from jax import lax
from jax.experimental import pallas as pl
from jax.experimental.pallas import tpu as pltpu

F32 = jnp.float32
BF16 = jnp.bfloat16

D_MODEL = 1024
D_FF = 2816
CONV_WIDTH = 512
CONV_K = 3
RWKV_WIDTH = 512
RWKV_HEAD = 64
RWKV_HEADS = RWKV_WIDTH // RWKV_HEAD
DECAY_LORA = 32
AAA_LORA = 32
GATE_LORA = 96
LORA_COLS = DECAY_LORA + AAA_LORA + GATE_LORA
N_ADA = 9
MACARON_W = 0.5
NORM_EPS = 1e-6
GN_EPS = 64e-5

LANES = 128
SUBLANES = 8
MXU_DIM = 256
LORA_PAD = MXU_DIM
HEADS_PER_PAIR = LANES // RWKV_HEAD
N_PAIRS = RWKV_WIDTH // LANES
CHUNK = 64
SUB = 16
MIX_TILE = 512
MIX_SUBTILE = 256
MIX_PHASES = 4
MIX_STEPS = 8
FFN_TILE = 1024
FFN_SUBTILE = 256
FFN_LAG = 4
FFN_COLS = MXU_DIM
ADA_COLS = 1024
VMEM_LIMIT = 56 * 1024 * 1024


def _rms(x):
    return x * lax.rsqrt(jnp.mean(x * x, axis=-1, keepdims=True) + NORM_EPS)


def _dot(a, b):
    return jnp.dot(a.astype(BF16), b.astype(BF16), preferred_element_type=F32)


def _ada_kernel(c_ref, w_ref, b_ref, o_ref):
    c = c_ref[...]
    cond = c * jax.nn.sigmoid(c)
    o_ref[0] = jnp.sum(w_ref[...] * cond, axis=0, keepdims=True) + b_ref[0]


def _ada(c, w_ada, b_ada):
    n_blk = (N_ADA * D_MODEL) // ADA_COLS
    return pl.pallas_call(
        _ada_kernel,
        grid=(n_blk,),
        in_specs=[
            pl.BlockSpec((D_MODEL, 1), lambda j: (0, 0)),
            pl.BlockSpec((D_MODEL, ADA_COLS), lambda j: (0, j)),
            pl.BlockSpec((1, 1, ADA_COLS), lambda j: (j, 0, 0)),
        ],
        out_specs=pl.BlockSpec((1, 1, ADA_COLS), lambda j: (j, 0, 0)),
        out_shape=jax.ShapeDtypeStruct((n_blk, 1, ADA_COLS), F32),
        compiler_params=pltpu.CompilerParams(dimension_semantics=("arbitrary",)),
        name="ada_proj",
    )(c.reshape(D_MODEL, 1), w_ada, b_ada.reshape(n_blk, 1, ADA_COLS))


def _ffn_rows_steps(h_ref, o_ref, act_ref, row0, rows, c):
    x = h_ref[row0:row0 + rows, :]
    u = (_rms(x) * c["g_pre"] * (1.0 + c["scale"]) + c["shift"]).astype(BF16)
    yield
    for j in range(D_FF // FFN_COLS):
        cols = slice(j * FFN_COLS, (j + 1) * FFN_COLS)
        a = jnp.dot(u, c["w1"][:, cols], preferred_element_type=F32)
        b = jnp.dot(u, c["w3"][:, cols], preferred_element_type=F32)
        act_ref[row0:row0 + rows, cols] = (a * jax.nn.sigmoid(a) * b).astype(BF16)
        yield
    act = act_ref[row0:row0 + rows, :]
    y_cols = []
    for j in range(D_MODEL // FFN_COLS):
        cols = slice(j * FFN_COLS, (j + 1) * FFN_COLS)
        y_cols.append(jnp.dot(act, c["w2"][:, cols], preferred_element_type=F32))
        yield
    y = jnp.concatenate(y_cols, axis=1)
    o_ref[row0:row0 + rows, :] = x + MACARON_W * c["gate"] * (_rms(y) * c["g_post"])
    yield


def _ffn_kernel(ada_row, h_ref, ada_ref, gpre_ref, w1_ref, w3_ref, w2_ref, gpost_ref,
                o_ref, act_ref):
    tile = h_ref.shape[0]
    sub = min(FFN_SUBTILE, tile)
    consts = dict(shift=ada_ref[ada_row], scale=ada_ref[ada_row + 1], gate=ada_ref[ada_row + 2],
                  g_pre=gpre_ref[...], g_post=gpost_ref[...], w1=w1_ref, w3=w3_ref, w2=w2_ref)
    blocks = [_ffn_rows_steps(h_ref, o_ref, act_ref, r0, sub, consts) for r0 in range(0, tile, sub)]
    n_steps = 2 + D_FF // FFN_COLS + D_MODEL // FFN_COLS
    for step in range((len(blocks) - 1) * FFN_LAG + n_steps):
        for k, blk in enumerate(blocks):
            if 0 <= step - k * FFN_LAG < n_steps:
                next(blk)


def _ffn(h, ada, ada_row, g_pre, w1, w3, w2, g_post):
    t = h.shape[0]
    tile = min(FFN_TILE, t)
    const = lambda i: (0, 0)
    return pl.pallas_call(
        functools.partial(_ffn_kernel, ada_row),
        grid=(t // tile,),
        in_specs=[
            pl.BlockSpec((tile, D_MODEL), lambda i: (i, 0)),
            pl.BlockSpec((N_ADA, 1, D_MODEL), lambda i: (0, 0, 0)),
            pl.BlockSpec((1, D_MODEL), const),
            pl.BlockSpec((D_MODEL, D_FF), const, pipeline_mode=pl.Buffered(1)),
            pl.BlockSpec((D_MODEL, D_FF), const, pipeline_mode=pl.Buffered(1)),
            pl.BlockSpec((D_FF, D_MODEL), const, pipeline_mode=pl.Buffered(1)),
            pl.BlockSpec((1, D_MODEL), const),
        ],
        out_specs=pl.BlockSpec((tile, D_MODEL), lambda i: (i, 0)),
        out_shape=jax.ShapeDtypeStruct((t, D_MODEL), F32),
        scratch_shapes=[pltpu.VMEM((tile, D_FF), BF16)],
        compiler_params=pltpu.CompilerParams(dimension_semantics=("arbitrary",),
                                             vmem_limit_bytes=VMEM_LIMIT),
        name="swiglu_half_step",
    )(h, ada, g_pre.reshape(1, D_MODEL), w1.astype(BF16), w3.astype(BF16), w2.astype(BF16),
      g_post.reshape(1, D_MODEL))


def _shift_rows(x, fill_row):
    row = lax.broadcasted_iota(jnp.int32, x.shape, 0)
    return jnp.where(row == 0, fill_row, pltpu.roll(x, 1, 0))


def _softplus(z):
    return jnp.maximum(z, 0.0) + jnp.log1p(jnp.exp(-jnp.abs(z)))


def _bmm(a, b):
    return jnp.einsum("bmk,bkn->bmn", a.astype(BF16), b.astype(BF16), preferred_element_type=F32)


def _bmm_nt(a, b):
    return jnp.einsum("bmk,bnk->bmn", a.astype(BF16), b.astype(BF16), preferred_element_type=F32)


def _bmm_tn(a, b):
    return jnp.einsum("bkm,bkn->bmn", a.astype(BF16), b.astype(BF16), preferred_element_type=F32)


def _block_diag(x, left):
    return jnp.concatenate([jnp.where(left, x, 0.0), jnp.where(left, 0.0, x)], axis=1)


def _chunk_terms_steps(ph, rh, qh, kh, pt, rt, qb, kb, v, masks, res):
    left, strict, incl, diag_blk, eye, bd_mask = masks
    bd = lambda x: _block_diag(x, left)
    mm = lambda a, b: _bmm(a, bd(b))
    gram = _bmm_nt(jnp.concatenate([ph, rh], axis=1),
                   jnp.concatenate([bd(qh), bd(kh)], axis=1))
    a_pq = jnp.where(strict, gram[:, :CHUNK, :LANES], 0.0)
    a_rq = jnp.where(incl, gram[:, CHUNK:, :LANES], 0.0)
    a_pk = jnp.where(strict, gram[:, :CHUNK, LANES:], 0.0)
    a_rk = jnp.where(incl, gram[:, CHUNK:, LANES:], 0.0)
    yield
    a_d = jnp.where(diag_blk, a_pq, 0.0)
    a_o = a_pq - a_d
    x2 = mm(a_d, a_d)
    t_d = eye + a_d
    yield
    both = mm(jnp.concatenate([x2, t_d], axis=1), x2)
    x4, t_d = both[:, :CHUNK], t_d + both[:, CHUNK:]
    yield
    both = mm(jnp.concatenate([x4, t_d], axis=1), x4)
    x8, t_d = both[:, :CHUNK], t_d + both[:, CHUNK:]
    yield
    t_d = t_d + mm(t_d, x8)
    n1 = mm(t_d, a_o)
    yield
    n2 = mm(n1, n1)
    s = eye + n1
    s = s + mm(s, n2)
    yield
    t_inv = mm(s, t_d)
    both = mm(jnp.concatenate([a_pk, a_rk], axis=1), v)
    av, ark_v = both[:, :CHUNK], both[:, CHUNK:]
    yield
    wu = _bmm(t_inv, jnp.concatenate([bd(pt), bd(av)], axis=2))
    w, ub = wu[:, :, :LANES], wu[:, :, LANES:]
    ru = _bmm(a_rq, jnp.concatenate([bd(w), bd(ub)], axis=2))
    res["rw"] = rt + ru[:, :, :LANES]
    res["y0"] = ru[:, :, LANES:] + ark_v
    res["g_off"] = jnp.where(bd_mask, _bmm_tn(qb, w), 0.0)
    res["c_bd"] = jnp.where(bd_mask, _bmm_tn(jnp.concatenate([qb, kb], axis=1),
                                             jnp.concatenate([ub, v], axis=1)), 0.0)
    yield


def _seg_sum(z, ones_bd):
    half = ones_bd.shape[0]
    return jnp.concatenate([_dot(z[:, :half], ones_bd), _dot(z[:, half:], ones_bd)], axis=1)


def _mix_rows_steps(k, h_ref, o_ref, row0, rows, shared, c):
    n_chunks = rows // CHUNK
    cw, rwd = CONV_WIDTH, RWKV_WIDTH
    x = h_ref[row0:row0 + rows, :]
    u = (_rms(x) * c["g_pre"] * (1.0 + c["scale"]) + c["shift"]).astype(BF16)
    yield
    p_cols = []
    for j in range(6):
        p_cols.append(jnp.dot(u, c["w_in"][:, j * cw:(j + 1) * cw], preferred_element_type=F32))
        yield
    plo = jnp.dot(u, c["w_lo"][...], preferred_element_type=F32)
    c_pre, c_post, c_val = p_cols[0], p_cols[1], p_cols[2]
    rw_raw = jnp.concatenate(p_cols[3:], axis=1)
    cv = c_pre * c_val
    shared[k] = dict(prev_rw=rw_raw[rows - 1:, :], prev_lo=plo[rows - 1:, :],
                     prev_cv1=cv[rows - 1:, :], prev_cv2=cv[rows - 2:rows - 1, :])
    yield

    prev = shared[k - 1]
    conv_w = c["conv_w"]
    cv1 = _shift_rows(cv, prev["prev_cv1"])
    cv2 = _shift_rows(cv1, prev["prev_cv2"])
    y_conv = c_post * (conv_w[0:1, :] * cv2 + conv_w[1:2, :] * cv1 + conv_w[2:3, :] * cv)
    yield
    rw_mix = rw_raw + (_shift_rows(rw_raw, prev["prev_rw"]) - rw_raw) * c["mu"]
    lo_mix = plo + (_shift_rows(plo, prev["prev_lo"]) - plo) * c["mu_lo"]
    xr, xk, xv = rw_mix[:, :rwd], rw_mix[:, rwd:2 * rwd], rw_mix[:, 2 * rwd:]
    yield
    lane_lo = lax.broadcasted_iota(jnp.int32, lo_mix.shape, 1)
    lo_act = jnp.where(lane_lo < DECAY_LORA, jnp.tanh(lo_mix),
                       jnp.where(lane_lo < DECAY_LORA + AAA_LORA, lo_mix, jax.nn.sigmoid(lo_mix)))
    lup = _dot(lo_act, c["lora_up"][...])
    yield
    w_raw = -_softplus(-(c["w0"] + lup[:, :rwd])) - 0.5
    logw = -jnp.exp(w_raw)
    a = jax.nn.sigmoid(c["a0"] + lup[:, rwd:2 * rwd])
    g = lup[:, 2 * rwd:]
    yield
    ones_bd = c["ones"][...]
    kk = xk * c["k_k"]
    kk = kk / jnp.maximum(jnp.sqrt(_seg_sum(kk * kk, ones_bd)), 1e-12)
    kmod = xk * (1.0 + (a - 1.0) * c["k_a"])
    bonus = _seg_sum(xr * kmod * c["r_k"], ones_bd) * xv
    pv = -kk
    qv = kk * a
    yield
    logw_hi = logw.astype(BF16)
    logw_lo = (logw - logw_hi.astype(F32)).astype(BF16)
    gc = (jnp.dot(c["dmat"], logw_hi, preferred_element_type=F32)
          + jnp.dot(c["dmat"], logw_lo, preferred_element_type=F32))
    e_fwd = jnp.exp(gc)
    e_bwd = jnp.exp(-gc)
    e_prev = jnp.exp(gc - logw)
    yield
    per_chunk = lambda z: z.reshape(n_chunks, CHUNK, rwd)
    gc3, lw3 = per_chunk(gc), per_chunk(logw)
    g_mid = lw3[:, 0:1, :] - gc3[:, 0:1, :]
    g_end = gc3[:, CHUNK - 1:CHUNK, :]
    e_mid = jnp.exp(g_mid)
    e_end = jnp.exp(g_end)
    decay_end = jnp.exp(g_mid + g_end)
    scale_chunk = lambda z, f: (per_chunk(z) * f).reshape(rows, rwd)
    ph = pv * e_prev
    rh = xr * e_fwd
    qh = qv * e_bwd
    kh = kmod * e_bwd
    yield
    pt = scale_chunk(ph, e_mid)
    rt = scale_chunk(rh, e_mid)
    qb = scale_chunk(qh, e_end)
    kb = scale_chunk(kh, e_end)

    def batch(z):
        return jnp.stack([z[ci * CHUNK:(ci + 1) * CHUNK, j * LANES:(j + 1) * LANES]
                          for ci in range(n_chunks) for j in range(N_PAIRS)], axis=0)

    terms = {}
    chunk_steps = _chunk_terms_steps(batch(ph), batch(rh), batch(qh), batch(kh), batch(pt),
                                     batch(rt), batch(qb), batch(kb), batch(xv), c["masks"], terms)
    yield

    for _ in chunk_steps:
        yield

    m = shared[k - 1]["state"]
    o_rows = []
    for ci in range(n_chunks):
        b = slice(ci * N_PAIRS, (ci + 1) * N_PAIRS)
        o_c = _bmm(terms["rw"][b], m) + terms["y0"][b]
        o_rows.append(jnp.concatenate([o_c[j] for j in range(N_PAIRS)], axis=1))
        dec = jnp.stack([jnp.where(c["eye2"], decay_end[ci, :, j * LANES:(j + 1) * LANES], 0.0)
                         for j in range(N_PAIRS)], axis=0)
        m = jnp.sum(dec, axis=2, keepdims=True) * m + _bmm(terms["g_off"][b], m) + terms["c_bd"][b]
        if ci == n_chunks - 1:
            shared[k]["state"] = m
        yield
    o = jnp.concatenate(o_rows, axis=0)
    inv_n = 1.0 / RWKV_HEAD
    o_c = o - _seg_sum(o, ones_bd) * inv_n
    yield
    o_var = _seg_sum(o_c * o_c, ones_bd) * inv_n
    o_n = o_c * lax.rsqrt(o_var + GN_EPS) * c["ln_w"] + c["ln_b"]
    y_rwkv = (o_n + bonus) * g
    yield
    y = _dot(jnp.concatenate([y_conv, y_rwkv], axis=1), c["w_out"][...])
    yield
    o_ref[row0:row0 + rows, :] = x + c["gate"] * (_rms(y) * c["g_post"])
    yield


def _mixer_kernel(h_ref, ada_ref, gpre_ref, win_ref, wlo_ref, mu_ref, mulo_ref, convw_ref,
                  w0_ref, a0_ref, loraup_ref, kk_ref, ka_ref, rk_ref, lnw_ref, lnb_ref,
                  ones_ref, wout_ref, gpost_ref, o_ref,
                  prev_rw_ref, prev_lo_ref, prev_cv_ref, state_ref):
    tile = h_ref.shape[0]
    sub = min(MIX_SUBTILE, tile)
    assert sub // CHUNK == MIX_STEPS // 2

    @pl.when(pl.program_id(0) == 0)
    def _():
        prev_rw_ref[...] = jnp.zeros_like(prev_rw_ref)
        prev_lo_ref[...] = jnp.zeros_like(prev_lo_ref)
        prev_cv_ref[...] = jnp.zeros_like(prev_cv_ref)
        state_ref[...] = jnp.zeros_like(state_ref)

    ti = lax.broadcasted_iota(jnp.int32, (sub, sub), 0)
    si = lax.broadcasted_iota(jnp.int32, (sub, sub), 1)
    tl, sl = ti % CHUNK, si % CHUNK
    mid = CHUNK // 2 - 1
    dmat = jnp.where((ti // CHUNK) == (si // CHUNK),
                     (sl <= tl).astype(F32) - (sl <= mid).astype(F32), 0.0).astype(BF16)
    r_i = lax.broadcasted_iota(jnp.int32, (CHUNK, LANES), 0)
    l_i = lax.broadcasted_iota(jnp.int32, (CHUNK, LANES), 1)
    col = l_i % RWKV_HEAD
    r2 = lax.broadcasted_iota(jnp.int32, (LANES, LANES), 0)
    c2 = lax.broadcasted_iota(jnp.int32, (LANES, LANES), 1)
    masks = (l_i < RWKV_HEAD, r_i > col, r_i >= col, (r_i // SUB) == (col // SUB),
             (r_i == col).astype(F32), (r2 // RWKV_HEAD) == (c2 // RWKV_HEAD))
    consts = dict(
        shift=ada_ref[3], scale=ada_ref[4], gate=ada_ref[5], g_pre=gpre_ref[...],
        w_in=win_ref, w_lo=wlo_ref, mu=mu_ref[...], mu_lo=mulo_ref[...], conv_w=convw_ref[...],
        w0=w0_ref[...], a0=a0_ref[...], lora_up=loraup_ref, k_k=kk_ref[...], k_a=ka_ref[...],
        r_k=rk_ref[...], ln_w=lnw_ref[...], ln_b=lnb_ref[...], ones=ones_ref, w_out=wout_ref,
        g_post=gpost_ref[...], dmat=dmat, masks=masks, eye2=r2 == c2)

    n_sub = tile // sub
    shared = {-1: dict(prev_rw=prev_rw_ref[...], prev_lo=prev_lo_ref[...],
                       prev_cv1=prev_cv_ref[0:1, :], prev_cv2=prev_cv_ref[1:2, :],
                       state=state_ref[...])}
    blocks = [_mix_rows_steps(k, h_ref, o_ref, k * sub, sub, shared, consts) for k in range(n_sub)]
    for step in range((n_sub - 1 + MIX_PHASES) * MIX_STEPS):
        for k, blk in enumerate(blocks):
            if 0 <= step - k * MIX_STEPS < MIX_PHASES * MIX_STEPS:
                next(blk)
    final = shared[n_sub - 1]
    prev_rw_ref[...] = final["prev_rw"]
    prev_lo_ref[...] = final["prev_lo"]
    prev_cv_ref[0:1, :] = final["prev_cv1"]
    prev_cv_ref[1:2, :] = final["prev_cv2"]
    state_ref[...] = final["state"]


def _mixer(h, ada, g_pre, w_in, conv_w, mu_shift, w0, w_up, a0, a_up, g_up, k_k, k_a, r_k,
           ln_x_w, ln_x_b, w_out, g_post):
    t = h.shape[0]
    tile = min(MIX_TILE, t)
    n_main = 3 * CONV_WIDTH + 3 * RWKV_WIDTH
    w_main = w_in[:, :n_main].astype(BF16)
    w_lo = jnp.pad(w_in[:, n_main:], ((0, 0), (0, LORA_PAD - LORA_COLS))).astype(BF16)
    mu_main = mu_shift[:3 * RWKV_WIDTH].reshape(1, -1)
    mu_lo = jnp.pad(mu_shift[3 * RWKV_WIDTH:], (0, LORA_PAD - LORA_COLS)).reshape(1, -1)
    lora_up = jnp.zeros((LORA_PAD, 3 * RWKV_WIDTH), F32)
    lora_up = lora_up.at[:DECAY_LORA, :RWKV_WIDTH].set(w_up)
    lora_up = lora_up.at[DECAY_LORA:DECAY_LORA + AAA_LORA, RWKV_WIDTH:2 * RWKV_WIDTH].set(a_up)
    lora_up = lora_up.at[DECAY_LORA + AAA_LORA:LORA_COLS, 2 * RWKV_WIDTH:].set(g_up)
    head_id = jnp.arange(MXU_DIM) // RWKV_HEAD
    ones_bd = (head_id[:, None] == head_id[None, :]).astype(BF16)
    row = lambda z: z.reshape(1, -1)
    const = lambda i: (0, 0)
    full = lambda shape: pl.BlockSpec(shape, const)
    return pl.pallas_call(
        _mixer_kernel,
        grid=(t // tile,),
        in_specs=[
            pl.BlockSpec((tile, D_MODEL), lambda i: (i, 0)),
            pl.BlockSpec((N_ADA, 1, D_MODEL), lambda i: (0, 0, 0)),
            full((1, D_MODEL)),
            full((D_MODEL, n_main)),
            full((D_MODEL, LORA_PAD)),
            full((1, 3 * RWKV_WIDTH)),
            full((1, LORA_PAD)),
            full((CONV_K, CONV_WIDTH)),
            full((1, RWKV_WIDTH)),
            full((1, RWKV_WIDTH)),
            full((LORA_PAD, 3 * RWKV_WIDTH)),
            full((1, RWKV_WIDTH)),
            full((1, RWKV_WIDTH)),
            full((1, RWKV_WIDTH)),
            full((1, RWKV_WIDTH)),
            full((1, RWKV_WIDTH)),
            full((MXU_DIM, MXU_DIM)),
            full((D_MODEL, D_MODEL)),
            full((1, D_MODEL)),
        ],
        out_specs=pl.BlockSpec((tile, D_MODEL), lambda i: (i, 0)),
        out_shape=jax.ShapeDtypeStruct((t, D_MODEL), F32),
        scratch_shapes=[
            pltpu.VMEM((1, 3 * RWKV_WIDTH), F32),
            pltpu.VMEM((1, LORA_PAD), F32),
            pltpu.VMEM((2, CONV_WIDTH), F32),
            pltpu.VMEM((N_PAIRS, LANES, LANES), F32),
        ],
        compiler_params=pltpu.CompilerParams(dimension_semantics=("arbitrary",),
                                             vmem_limit_bytes=VMEM_LIMIT),
        name="token_mixing",
    )(h, ada, row(g_pre), w_main, w_lo, mu_main, mu_lo, conv_w, row(w0), row(a0),
      lora_up.astype(BF16), row(k_k), row(k_a), row(r_k), row(ln_x_w), row(ln_x_b), ones_bd,
      w_out.astype(BF16), row(g_post))


def kernel(x, c, w_ada, b_ada, ffn1_g_pre, ffn1_w1, ffn1_w3, ffn1_w2, ffn1_g_post, mix_g_pre, w_in, conv_w, mu_shift, w0, w_up, a0, a_up, g_up, k_k, k_a, r_k, ln_x_w, ln_x_b, w_out, mix_g_post, ffn2_g_pre, ffn2_w1, ffn2_w3, ffn2_w2, ffn2_g_post):
    bsz, t, _ = x.shape
    outs = []
    for b in range(bsz):
        h = x[b]
        for l in range(w_ada.shape[0]):
            ada = _ada(c[b], w_ada[l], b_ada[l])
            h = _ffn(h, ada, 0, ffn1_g_pre[l], ffn1_w1[l], ffn1_w3[l], ffn1_w2[l], ffn1_g_post[l])
            h = _mixer(h, ada, mix_g_pre[l], w_in[l], conv_w[l], mu_shift[l], w0[l], w_up[l],
                       a0[l], a_up[l], g_up[l], k_k[l], k_a[l], r_k[l], ln_x_w[l], ln_x_b[l],
                       w_out[l], mix_g_post[l])
            h = _ffn(h, ada, 6, ffn2_g_pre[l], ffn2_w1[l], ffn2_w3[l], ffn2_w2[l], ffn2_g_post[l])
        outs.append(h)
    return jnp.stack(outs, axis=0)
```

```python
import functools

import jax
import jax.numpy as jnp
from jax import lax
from jax.experimental import pallas as pl
from jax.experimental.pallas import tpu as pltpu

F32 = jnp.float32
BF16 = jnp.bfloat16

D_MODEL = 1024
D_FF = 2816
CONV_WIDTH = 512
CONV_K = 3
RWKV_WIDTH = 512
RWKV_HEAD = 64
RWKV_HEADS = RWKV_WIDTH // RWKV_HEAD
DECAY_LORA = 32
AAA_LORA = 32
GATE_LORA = 96
LORA_COLS = DECAY_LORA + AAA_LORA + GATE_LORA
N_ADA = 9
MACARON_W = 0.5
NORM_EPS = 1e-6
GN_EPS = 64e-5

LANES = 128
SUBLANES = 8
MXU_DIM = 256
LORA_PAD = MXU_DIM
BF16_ROWS = 16
MIX_MAIN_COLS = 3 * CONV_WIDTH + 3 * RWKV_WIDTH
MIX_INPUTS = 19
HEADS_PER_PAIR = LANES // RWKV_HEAD
N_PAIRS = RWKV_WIDTH // LANES
CHUNK = 64
SUB = 16
MIX_TILE = 512
MIX_SUBTILE = 256
MIX_PHASES = 4
MIX_STEPS = 8
FFN_TILE = 1024
FFN_SUBTILE = 256
FFN_LAG = 4
FFN_COLS = MXU_DIM
ADA_COLS = 1024
VMEM_LIMIT = 56 * 1024 * 1024


def _rms(x):
    return x * lax.rsqrt(jnp.mean(x * x, axis=-1, keepdims=True) + NORM_EPS)


def _dot(a, b):
    return jnp.dot(a.astype(BF16), b.astype(BF16), preferred_element_type=F32)


def _ada_kernel(c_ref, w_ref, b_ref, o_ref):
    c = c_ref[...]
    cond = c * jax.nn.sigmoid(c)
    o_ref[0] = jnp.sum(w_ref[...] * cond, axis=0, keepdims=True) + b_ref[0]


def _ada(c, w_ada, b_ada):
    n_blk = (N_ADA * D_MODEL) // ADA_COLS
    return pl.pallas_call(
        _ada_kernel,
        grid=(n_blk,),
        in_specs=[
            pl.BlockSpec((D_MODEL, 1), lambda j: (0, 0)),
            pl.BlockSpec((D_MODEL, ADA_COLS), lambda j: (0, j)),
            pl.BlockSpec((1, 1, ADA_COLS), lambda j: (j, 0, 0)),
        ],
        out_specs=pl.BlockSpec((1, 1, ADA_COLS), lambda j: (j, 0, 0)),
        out_shape=jax.ShapeDtypeStruct((n_blk, 1, ADA_COLS), F32),
        compiler_params=pltpu.CompilerParams(dimension_semantics=("arbitrary",)),
        name="ada_proj",
    )(c.reshape(D_MODEL, 1), w_ada, b_ada.reshape(n_blk, 1, ADA_COLS))


def _ffn_rows_steps(h_ref, o_ref, act_ref, row0, rows, c):
    x = h_ref[row0:row0 + rows, :]
    u = (_rms(x) * c["g_pre"] * (1.0 + c["scale"]) + c["shift"]).astype(BF16)
    yield
    for j in range(D_FF // FFN_COLS):
        cols = slice(j * FFN_COLS, (j + 1) * FFN_COLS)
        a = jnp.dot(u, c["w1"][:, cols], preferred_element_type=F32)
        b = jnp.dot(u, c["w3"][:, cols], preferred_element_type=F32)
        act_ref[row0:row0 + rows, cols] = (a * jax.nn.sigmoid(a) * b).astype(BF16)
        yield
    act = act_ref[row0:row0 + rows, :]
    y_cols = []
    for j in range(D_MODEL // FFN_COLS):
        cols = slice(j * FFN_COLS, (j + 1) * FFN_COLS)
        y_cols.append(jnp.dot(act, c["w2"][:, cols], preferred_element_type=F32))
        yield
    y = jnp.concatenate(y_cols, axis=1)
    o_ref[row0:row0 + rows, :] = x + MACARON_W * c["gate"] * (_rms(y) * c["g_post"])
    yield


def _cast_plan(casts, n_steps):
    in_specs, out_specs, out_shapes = [], [], []
    for arr, cols in casts:
        rows = arr.shape[0]
        n_blk = n_steps
        while rows % n_blk or (rows // n_blk) % BF16_ROWS:
            n_blk //= 2
        idx = lambda i, every=n_steps // n_blk: (i // every, 0)
        in_specs.append(pl.BlockSpec((rows // n_blk, arr.shape[1]), idx))
        out_specs.append(pl.BlockSpec((rows // n_blk, cols), idx))
        out_shapes.append(jax.ShapeDtypeStruct((rows, cols), BF16))
    return in_specs, out_specs, out_shapes


def _cast_blocks(src_refs, dst_refs):
    for src, dst in zip(src_refs, dst_refs):
        dst[...] = src[:, :dst.shape[1]].astype(BF16)


def _ffn_kernel(ada_row, n_cast, *refs):
    h_ref, ada_ref, gpre_ref, w1_ref, w3_ref, w2_ref, gpost_ref = refs[:7]
    o_ref, act_ref = refs[7 + n_cast], refs[-1]
    _cast_blocks(refs[7:7 + n_cast], refs[8 + n_cast:8 + 2 * n_cast])
    tile = h_ref.shape[0]
    sub = min(FFN_SUBTILE, tile)
    consts = dict(shift=ada_ref[ada_row], scale=ada_ref[ada_row + 1], gate=ada_ref[ada_row + 2],
                  g_pre=gpre_ref[...], g_post=gpost_ref[...], w1=w1_ref, w3=w3_ref, w2=w2_ref)
    blocks = [_ffn_rows_steps(h_ref, o_ref, act_ref, r0, sub, consts) for r0 in range(0, tile, sub)]
    n_steps = 2 + D_FF // FFN_COLS + D_MODEL // FFN_COLS
    for step in range((len(blocks) - 1) * FFN_LAG + n_steps):
        for k, blk in enumerate(blocks):
            if 0 <= step - k * FFN_LAG < n_steps:
                next(blk)


def _ffn(h, ada, ada_row, g_pre, w1, w3, w2, g_post, casts=()):
    t = h.shape[0]
    tile = min(FFN_TILE, t)
    const = lambda i: (0, 0)
    cast_in, cast_out, cast_shapes = _cast_plan(casts, t // tile)
    out = pl.pallas_call(
        functools.partial(_ffn_kernel, ada_row, len(casts)),
        grid=(t // tile,),
        in_specs=[
            pl.BlockSpec((tile, D_MODEL), lambda i: (i, 0)),
            pl.BlockSpec((N_ADA, 1, D_MODEL), lambda i: (0, 0, 0)),
            pl.BlockSpec((1, D_MODEL), const),
            pl.BlockSpec((D_MODEL, D_FF), const, pipeline_mode=pl.Buffered(1)),
            pl.BlockSpec((D_MODEL, D_FF), const, pipeline_mode=pl.Buffered(1)),
            pl.BlockSpec((D_FF, D_MODEL), const, pipeline_mode=pl.Buffered(1)),
            pl.BlockSpec((1, D_MODEL), const),
        ] + cast_in,
        out_specs=[pl.BlockSpec((tile, D_MODEL), lambda i: (i, 0))] + cast_out,
        out_shape=[jax.ShapeDtypeStruct((t, D_MODEL), F32)] + cast_shapes,
        scratch_shapes=[pltpu.VMEM((tile, D_FF), BF16)],
        compiler_params=pltpu.CompilerParams(dimension_semantics=("arbitrary",),
                                             vmem_limit_bytes=VMEM_LIMIT),
        name="swiglu_half_step",
    )(h, ada, g_pre.reshape(1, D_MODEL), w1.astype(BF16), w3.astype(BF16), w2.astype(BF16),
      g_post.reshape(1, D_MODEL), *[arr for arr, _ in casts])
    return out[0], out[1:]


def _shift_rows(x, fill_row):
    row = lax.broadcasted_iota(jnp.int32, x.shape, 0)
    return jnp.where(row == 0, fill_row, pltpu.roll(x, 1, 0))


def _softplus(z):
    return jnp.maximum(z, 0.0) + jnp.log1p(jnp.exp(-jnp.abs(z)))


def _bmm(a, b):
    return jnp.einsum("bmk,bkn->bmn", a.astype(BF16), b.astype(BF16), preferred_element_type=F32)


def _bmm_nt(a, b):
    return jnp.einsum("bmk,bnk->bmn", a.astype(BF16), b.astype(BF16), preferred_element_type=F32)


def _bmm_tn(a, b):
    return jnp.einsum("bkm,bkn->bmn", a.astype(BF16), b.astype(BF16), preferred_element_type=F32)


def _block_diag(x, left):
    return jnp.concatenate([jnp.where(left, x, 0.0), jnp.where(left, 0.0, x)], axis=1)


def _chunk_terms_steps(ph, rh, qh, kh, pt, rt, qb, kb, v, masks, res):
    left, strict, incl, diag_blk, eye, bd_mask = masks
    bd = lambda x: _block_diag(x, left)
    mm = lambda a, b: _bmm(a, bd(b))
    gram = _bmm_nt(jnp.concatenate([ph, rh], axis=1),
                   jnp.concatenate([bd(qh), bd(kh)], axis=1))
    a_pq = jnp.where(strict, gram[:, :CHUNK, :LANES], 0.0)
    a_rq = jnp.where(incl, gram[:, CHUNK:, :LANES], 0.0)
    a_pk = jnp.where(strict, gram[:, :CHUNK, LANES:], 0.0)
    a_rk = jnp.where(incl, gram[:, CHUNK:, LANES:], 0.0)
    yield
    a_d = jnp.where(diag_blk, a_pq, 0.0)
    a_o = a_pq - a_d
    x2 = mm(a_d, a_d)
    t_d = eye + a_d
    yield
    both = mm(jnp.concatenate([x2, t_d], axis=1), x2)
    x4, t_d = both[:, :CHUNK], t_d + both[:, CHUNK:]
    yield
    both = mm(jnp.concatenate([x4, t_d], axis=1), x4)
    x8, t_d = both[:, :CHUNK], t_d + both[:, CHUNK:]
    yield
    t_d = t_d + mm(t_d, x8)
    n1 = mm(t_d, a_o)
    yield
    n2 = mm(n1, n1)
    s = eye + n1
    s = s + mm(s, n2)
    yield
    t_inv = mm(s, t_d)
    both = mm(jnp.concatenate([a_pk, a_rk], axis=1), v)
    av, ark_v = both[:, :CHUNK], both[:, CHUNK:]
    yield
    wu = _bmm(t_inv, jnp.concatenate([bd(pt), bd(av)], axis=2))
    w, ub = wu[:, :, :LANES], wu[:, :, LANES:]
    ru = _bmm(a_rq, jnp.concatenate([bd(w), bd(ub)], axis=2))
    res["rw"] = rt + ru[:, :, :LANES]
    res["y0"] = ru[:, :, LANES:] + ark_v
    res["g_off"] = jnp.where(bd_mask, _bmm_tn(qb, w), 0.0)
    res["c_bd"] = jnp.where(bd_mask, _bmm_tn(jnp.concatenate([qb, kb], axis=1),
                                             jnp.concatenate([ub, v], axis=1)), 0.0)
    yield


def _seg_sum(z, ones_bd):
    half = ones_bd.shape[0]
    return jnp.concatenate([_dot(z[:, :half], ones_bd), _dot(z[:, half:], ones_bd)], axis=1)


def _mix_rows_steps(k, h_ref, o_ref, row0, rows, shared, c):
    n_chunks = rows // CHUNK
    cw, rwd = CONV_WIDTH, RWKV_WIDTH
    x = h_ref[row0:row0 + rows, :]
    u = (_rms(x) * c["g_pre"] * (1.0 + c["scale"]) + c["shift"]).astype(BF16)
    yield
    p_cols = []
    for j in range(6):
        p_cols.append(jnp.dot(u, c["w_in"][:, j * cw:(j + 1) * cw], preferred_element_type=F32))
        yield
    plo = jnp.dot(u, c["w_lo"][...], preferred_element_type=F32)
    c_pre, c_post, c_val = p_cols[0], p_cols[1], p_cols[2]
    rw_raw = jnp.concatenate(p_cols[3:], axis=1)
    cv = c_pre * c_val
    shared[k] = dict(prev_rw=rw_raw[rows - 1:, :], prev_lo=plo[rows - 1:, :],
                     prev_cv1=cv[rows - 1:, :], prev_cv2=cv[rows - 2:rows - 1, :])
    yield

    prev = shared[k - 1]
    conv_w = c["conv_w"]
    cv1 = _shift_rows(cv, prev["prev_cv1"])
    cv2 = _shift_rows(cv1, prev["prev_cv2"])
    y_conv = c_post * (conv_w[0:1, :] * cv2 + conv_w[1:2, :] * cv1 + conv_w[2:3, :] * cv)
    yield
    rw_mix = rw_raw + (_shift_rows(rw_raw, prev["prev_rw"]) - rw_raw) * c["mu"]
    lo_mix = plo + (_shift_rows(plo, prev["prev_lo"]) - plo) * c["mu_lo"]
    xr, xk, xv = rw_mix[:, :rwd], rw_mix[:, rwd:2 * rwd], rw_mix[:, 2 * rwd:]
    yield
    lane_lo = lax.broadcasted_iota(jnp.int32, lo_mix.shape, 1)
    lo_act = jnp.where(lane_lo < DECAY_LORA, jnp.tanh(lo_mix),
                       jnp.where(lane_lo < DECAY_LORA + AAA_LORA, lo_mix, jax.nn.sigmoid(lo_mix)))
    lup = _dot(lo_act, c["lora_up"][...])
    yield
    w_raw = -_softplus(-(c["w0"] + lup[:, :rwd])) - 0.5
    logw = -jnp.exp(w_raw)
    a = jax.nn.sigmoid(c["a0"] + lup[:, rwd:2 * rwd])
    g = lup[:, 2 * rwd:]
    yield
    ones_bd = c["ones"][...]
    kk = xk * c["k_k"]
    kk = kk / jnp.maximum(jnp.sqrt(_seg_sum(kk * kk, ones_bd)), 1e-12)
    kmod = xk * (1.0 + (a - 1.0) * c["k_a"])
    bonus = _seg_sum(xr * kmod * c["r_k"], ones_bd) * xv
    pv = -kk
    qv = kk * a
    yield
    logw_hi = logw.astype(BF16)
    logw_lo = (logw - logw_hi.astype(F32)).astype(BF16)
    gc = (jnp.dot(c["dmat"], logw_hi, preferred_element_type=F32)
          + jnp.dot(c["dmat"], logw_lo, preferred_element_type=F32))
    e_fwd = jnp.exp(gc)
    e_bwd = jnp.exp(-gc)
    e_prev = jnp.exp(gc - logw)
    yield
    per_chunk = lambda z: z.reshape(n_chunks, CHUNK, rwd)
    gc3, lw3 = per_chunk(gc), per_chunk(logw)
    g_mid = lw3[:, 0:1, :] - gc3[:, 0:1, :]
    g_end = gc3[:, CHUNK - 1:CHUNK, :]
    e_mid = jnp.exp(g_mid)
    e_end = jnp.exp(g_end)
    decay_end = jnp.exp(g_mid + g_end)
    scale_chunk = lambda z, f: (per_chunk(z) * f).reshape(rows, rwd)
    ph = pv * e_prev
    rh = xr * e_fwd
    qh = qv * e_bwd
    kh = kmod * e_bwd
    yield
    pt = scale_chunk(ph, e_mid)
    rt = scale_chunk(rh, e_mid)
    qb = scale_chunk(qh, e_end)
    kb = scale_chunk(kh, e_end)

    def batch(z):
        return jnp.stack([z[ci * CHUNK:(ci + 1) * CHUNK, j * LANES:(j + 1) * LANES]
                          for ci in range(n_chunks) for j in range(N_PAIRS)], axis=0)

    terms = {}
    chunk_steps = _chunk_terms_steps(batch(ph), batch(rh), batch(qh), batch(kh), batch(pt),
                                     batch(rt), batch(qb), batch(kb), batch(xv), c["masks"], terms)
    yield

    for _ in chunk_steps:
        yield

    m = shared[k - 1]["state"]
    o_rows = []
    for ci in range(n_chunks):
        b = slice(ci * N_PAIRS, (ci + 1) * N_PAIRS)
        o_c = _bmm(terms["rw"][b], m) + terms["y0"][b]
        o_rows.append(jnp.concatenate([o_c[j] for j in range(N_PAIRS)], axis=1))
        dec = jnp.stack([jnp.where(c["eye2"], decay_end[ci, :, j * LANES:(j + 1) * LANES], 0.0)
                         for j in range(N_PAIRS)], axis=0)
        m = jnp.sum(dec, axis=2, keepdims=True) * m + _bmm(terms["g_off"][b], m) + terms["c_bd"][b]
        if ci == n_chunks - 1:
            shared[k]["state"] = m
        yield
    o = jnp.concatenate(o_rows, axis=0)
    inv_n = 1.0 / RWKV_HEAD
    o_c = o - _seg_sum(o, ones_bd) * inv_n
    yield
    o_var = _seg_sum(o_c * o_c, ones_bd) * inv_n
    o_n = o_c * lax.rsqrt(o_var + GN_EPS) * c["ln_w"] + c["ln_b"]
    y_rwkv = (o_n + bonus) * g
    yield
    y = _dot(jnp.concatenate([y_conv, y_rwkv], axis=1), c["w_out"][...])
    yield
    o_ref[row0:row0 + rows, :] = x + c["gate"] * (_rms(y) * c["g_post"])
    yield


def _mixer_kernel(n_cast, *refs):
    (h_ref, ada_ref, gpre_ref, win_ref, wlo_ref, mu_ref, mulo_ref, convw_ref, w0_ref, a0_ref,
     loraup_ref, kk_ref, ka_ref, rk_ref, lnw_ref, lnb_ref, ones_ref, wout_ref,
     gpost_ref) = refs[:MIX_INPUTS]
    o_ref = refs[MIX_INPUTS + n_cast]
    prev_rw_ref, prev_lo_ref, prev_cv_ref, state_ref = refs[-4:]
    _cast_blocks(refs[MIX_INPUTS:MIX_INPUTS + n_cast],
                 refs[MIX_INPUTS + n_cast + 1:MIX_INPUTS + 2 * n_cast + 1])
    tile = h_ref.shape[0]
    sub = min(MIX_SUBTILE, tile)
    assert sub // CHUNK == MIX_STEPS // 2

    @pl.when(pl.program_id(0) == 0)
    def _():
        prev_rw_ref[...] = jnp.zeros_like(prev_rw_ref)
        prev_lo_ref[...] = jnp.zeros_like(prev_lo_ref)
        prev_cv_ref[...] = jnp.zeros_like(prev_cv_ref)
        state_ref[...] = jnp.zeros_like(state_ref)

    ti = lax.broadcasted_iota(jnp.int32, (sub, sub), 0)
    si = lax.broadcasted_iota(jnp.int32, (sub, sub), 1)
    tl, sl = ti % CHUNK, si % CHUNK
    mid = CHUNK // 2 - 1
    dmat = jnp.where((ti // CHUNK) == (si // CHUNK),
                     (sl <= tl).astype(F32) - (sl <= mid).astype(F32), 0.0).astype(BF16)
    r_i = lax.broadcasted_iota(jnp.int32, (CHUNK, LANES), 0)
    l_i = lax.broadcasted_iota(jnp.int32, (CHUNK, LANES), 1)
    col = l_i % RWKV_HEAD
    r2 = lax.broadcasted_iota(jnp.int32, (LANES, LANES), 0)
    c2 = lax.broadcasted_iota(jnp.int32, (LANES, LANES), 1)
    masks = (l_i < RWKV_HEAD, r_i > col, r_i >= col, (r_i // SUB) == (col // SUB),
             (r_i == col).astype(F32), (r2 // RWKV_HEAD) == (c2 // RWKV_HEAD))
    consts = dict(
        shift=ada_ref[3], scale=ada_ref[4], gate=ada_ref[5], g_pre=gpre_ref[...],
        w_in=win_ref, w_lo=wlo_ref, mu=mu_ref[...], mu_lo=mulo_ref[...], conv_w=convw_ref[...],
        w0=w0_ref[...], a0=a0_ref[...], lora_up=loraup_ref, k_k=kk_ref[...], k_a=ka_ref[...],
        r_k=rk_ref[...], ln_w=lnw_ref[...], ln_b=lnb_ref[...], ones=ones_ref, w_out=wout_ref,
        g_post=gpost_ref[...], dmat=dmat, masks=masks, eye2=r2 == c2)

    n_sub = tile // sub
    shared = {-1: dict(prev_rw=prev_rw_ref[...], prev_lo=prev_lo_ref[...],
                       prev_cv1=prev_cv_ref[0:1, :], prev_cv2=prev_cv_ref[1:2, :],
                       state=state_ref[...])}
    blocks = [_mix_rows_steps(k, h_ref, o_ref, k * sub, sub, shared, consts) for k in range(n_sub)]
    for step in range((n_sub - 1 + MIX_PHASES) * MIX_STEPS):
        for k, blk in enumerate(blocks):
            if 0 <= step - k * MIX_STEPS < MIX_PHASES * MIX_STEPS:
                next(blk)
    final = shared[n_sub - 1]
    prev_rw_ref[...] = final["prev_rw"]
    prev_lo_ref[...] = final["prev_lo"]
    prev_cv_ref[0:1, :] = final["prev_cv1"]
    prev_cv_ref[1:2, :] = final["prev_cv2"]
    state_ref[...] = final["state"]


def _mixer(h, ada, g_pre, w_main, w_in, conv_w, mu_shift, w0, w_up, a0, a_up, g_up, k_k, k_a, r_k,
           ln_x_w, ln_x_b, w_out, g_post, casts=()):
    t = h.shape[0]
    tile = min(MIX_TILE, t)
    n_main = MIX_MAIN_COLS
    w_lo = jnp.pad(w_in[:, n_main:], ((0, 0), (0, LORA_PAD - LORA_COLS))).astype(BF16)
    mu_main = mu_shift[:3 * RWKV_WIDTH].reshape(1, -1)
    mu_lo = jnp.pad(mu_shift[3 * RWKV_WIDTH:], (0, LORA_PAD - LORA_COLS)).reshape(1, -1)
    lora_up = jnp.zeros((LORA_PAD, 3 * RWKV_WIDTH), F32)
    lora_up = lora_up.at[:DECAY_LORA, :RWKV_WIDTH].set(w_up)
    lora_up = lora_up.at[DECAY_LORA:DECAY_LORA + AAA_LORA, RWKV_WIDTH:2 * RWKV_WIDTH].set(a_up)
    lora_up = lora_up.at[DECAY_LORA + AAA_LORA:LORA_COLS, 2 * RWKV_WIDTH:].set(g_up)
    head_id = jnp.arange(MXU_DIM) // RWKV_HEAD
    ones_bd = (head_id[:, None] == head_id[None, :]).astype(BF16)
    row = lambda z: z.reshape(1, -1)
    const = lambda i: (0, 0)
    full = lambda shape: pl.BlockSpec(shape, const)
    cast_in, cast_out, cast_shapes = _cast_plan(casts, t // tile)
    out = pl.pallas_call(
        functools.partial(_mixer_kernel, len(casts)),
        grid=(t // tile,),
        in_specs=[
            pl.BlockSpec((tile, D_MODEL), lambda i: (i, 0)),
            pl.BlockSpec((N_ADA, 1, D_MODEL), lambda i: (0, 0, 0)),
            full((1, D_MODEL)),
            full((D_MODEL, n_main)),
            full((D_MODEL, LORA_PAD)),
            full((1, 3 * RWKV_WIDTH)),
            full((1, LORA_PAD)),
            full((CONV_K, CONV_WIDTH)),
            full((1, RWKV_WIDTH)),
            full((1, RWKV_WIDTH)),
            full((LORA_PAD, 3 * RWKV_WIDTH)),
            full((1, RWKV_WIDTH)),
            full((1, RWKV_WIDTH)),
            full((1, RWKV_WIDTH)),
            full((1, RWKV_WIDTH)),
            full((1, RWKV_WIDTH)),
            full((MXU_DIM, MXU_DIM)),
            full((D_MODEL, D_MODEL)),
            full((1, D_MODEL)),
        ] + cast_in,
        out_specs=[pl.BlockSpec((tile, D_MODEL), lambda i: (i, 0))] + cast_out,
        out_shape=[jax.ShapeDtypeStruct((t, D_MODEL), F32)] + cast_shapes,
        scratch_shapes=[
            pltpu.VMEM((1, 3 * RWKV_WIDTH), F32),
            pltpu.VMEM((1, LORA_PAD), F32),
            pltpu.VMEM((2, CONV_WIDTH), F32),
            pltpu.VMEM((N_PAIRS, LANES, LANES), F32),
        ],
        compiler_params=pltpu.CompilerParams(dimension_semantics=("arbitrary",),
                                             vmem_limit_bytes=VMEM_LIMIT),
        name="token_mixing",
    )(h, ada, row(g_pre), w_main, w_lo, mu_main, mu_lo, conv_w, row(w0), row(a0),
      lora_up.astype(BF16), row(k_k), row(k_a), row(r_k), row(ln_x_w), row(ln_x_b), ones_bd,
      w_out, row(g_post), *[arr for arr, _ in casts])
    return out[0], out[1:]


def kernel(x, c, w_ada, b_ada, ffn1_g_pre, ffn1_w1, ffn1_w3, ffn1_w2, ffn1_g_post, mix_g_pre, w_in, conv_w, mu_shift, w0, w_up, a0, a_up, g_up, k_k, k_a, r_k, ln_x_w, ln_x_b, w_out, mix_g_post, ffn2_g_pre, ffn2_w1, ffn2_w3, ffn2_w2, ffn2_g_post):
    bsz, t, _ = x.shape
    outs = []
    for b in range(bsz):
        h = x[b]
        for l in range(w_ada.shape[0]):
            ada = _ada(c[b], w_ada[l], b_ada[l])
            h, (w_main, w_out_b) = _ffn(
                h, ada, 0, ffn1_g_pre[l], ffn1_w1[l], ffn1_w3[l], ffn1_w2[l], ffn1_g_post[l],
                casts=[(w_in[l], MIX_MAIN_COLS), (w_out[l], D_MODEL)])
            h, (w1_b, w3_b, w2_b) = _mixer(
                h, ada, mix_g_pre[l], w_main, w_in[l], conv_w[l], mu_shift[l], w0[l], w_up[l],
                a0[l], a_up[l], g_up[l], k_k[l], k_a[l], r_k[l], ln_x_w[l], ln_x_b[l],
                w_out_b, mix_g_post[l],
                casts=[(ffn2_w1[l], D_FF), (ffn2_w3[l], D_FF), (ffn2_w2[l], D_MODEL)])
            h, _ = _ffn(h, ada, 6, ffn2_g_pre[l], w1_b, w3_b, w2_b, ffn2_g_post[l])
        outs.append(h)
    return jnp.stack(outs, axis=0)
```

```python
import functools

import jax
import jax.numpy as jnp
from jax import lax
from jax.experimental import pallas as pl
from jax.experimental.pallas import tpu as pltpu

F32 = jnp.float32
BF16 = jnp.bfloat16

D_MODEL = 1024
D_FF = 2816
CONV_WIDTH = 512
CONV_K = 3
RWKV_WIDTH = 512
RWKV_HEAD = 64
RWKV_HEADS = RWKV_WIDTH // RWKV_HEAD
DECAY_LORA = 32
AAA_LORA = 32
GATE_LORA = 96
LORA_COLS = DECAY_LORA + AAA_LORA + GATE_LORA
N_ADA = 9
MACARON_W = 0.5
NORM_EPS = 1e-6
GN_EPS = 64e-5

LANES = 128
SUBLANES = 8
MXU_DIM = 256
LORA_PAD = MXU_DIM
BF16_ROWS = 16
MIX_MAIN_COLS = 3 * CONV_WIDTH + 3 * RWKV_WIDTH
MIX_INPUTS = 19
HEADS_PER_PAIR = LANES // RWKV_HEAD
N_PAIRS = RWKV_WIDTH // LANES
CHUNK = 64
SUB = 16
MIX_TILE = 512
MIX_SUBTILE = 256
MIX_PHASES = 5
MIX_LAG = 8
MIX_STEPS = 8
FFN_TILE = 1024
FFN_SUBTILE = 256
FFN_LAG = 4
FFN_COLS = MXU_DIM
ADA_COLS = 512
VMEM_LIMIT = 56 * 1024 * 1024


def _rms(x):
    return x * lax.rsqrt(jnp.mean(x * x, axis=-1, keepdims=True) + NORM_EPS)


def _dot(a, b):
    return jnp.dot(a.astype(BF16), b.astype(BF16), preferred_element_type=F32)


def _ada_kernel(c_ref, w_ref, b_ref, o_ref):
    c = c_ref[...]
    cond = c * jax.nn.sigmoid(c)
    o_ref[0] = jnp.sum(w_ref[...] * cond, axis=0, keepdims=True) + b_ref[0]


def _ada(c, w_ada, b_ada):
    per_row = D_MODEL // ADA_COLS
    vec = pl.BlockSpec((1, 1, ADA_COLS), lambda j: (j // per_row, 0, j % per_row))
    return pl.pallas_call(
        _ada_kernel,
        grid=(N_ADA * per_row,),
        in_specs=[
            pl.BlockSpec((D_MODEL, 1), lambda j: (0, 0)),
            pl.BlockSpec((D_MODEL, ADA_COLS), lambda j: (0, j)),
            vec,
        ],
        out_specs=vec,
        out_shape=jax.ShapeDtypeStruct((N_ADA, 1, D_MODEL), F32),
        compiler_params=pltpu.CompilerParams(dimension_semantics=("arbitrary",)),
        name="ada_proj",
    )(c.reshape(D_MODEL, 1), w_ada, b_ada.reshape(N_ADA, 1, D_MODEL))


def _ffn_rows_steps(h_ref, o_ref, act_ref, row0, rows, c):
    x = h_ref[row0:row0 + rows, :]
    u = (_rms(x) * c["g_pre"] * (1.0 + c["scale"]) + c["shift"]).astype(BF16)
    yield
    for j in range(D_FF // FFN_COLS):
        cols = slice(j * FFN_COLS, (j + 1) * FFN_COLS)
        a = jnp.dot(u, c["w1"][:, cols], preferred_element_type=F32)
        b = jnp.dot(u, c["w3"][:, cols], preferred_element_type=F32)
        act_ref[row0:row0 + rows, cols] = (a * jax.nn.sigmoid(a) * b).astype(BF16)
        yield
    act = act_ref[row0:row0 + rows, :]
    y_cols = []
    for j in range(D_MODEL // FFN_COLS):
        cols = slice(j * FFN_COLS, (j + 1) * FFN_COLS)
        y_cols.append(jnp.dot(act, c["w2"][:, cols], preferred_element_type=F32))
        yield
    y = jnp.concatenate(y_cols, axis=1)
    o_ref[row0:row0 + rows, :] = x + MACARON_W * c["gate"] * (_rms(y) * c["g_post"])
    yield


def _cast_plan(casts, n_steps):
    in_specs, out_specs, out_shapes = [], [], []
    for arr, cols in casts:
        rows = arr.shape[0]
        n_blk = n_steps
        while rows % n_blk or (rows // n_blk) % BF16_ROWS:
            n_blk //= 2
        idx = lambda i, every=n_steps // n_blk: (i // every, 0)
        in_specs.append(pl.BlockSpec((rows // n_blk, arr.shape[1]), idx))
        out_specs.append(pl.BlockSpec((rows // n_blk, cols), idx))
        out_shapes.append(jax.ShapeDtypeStruct((rows, cols), BF16))
    return in_specs, out_specs, out_shapes


def _cast_blocks(src_refs, dst_refs):
    for src, dst in zip(src_refs, dst_refs):
        dst[...] = src[:, :dst.shape[1]].astype(BF16)


def _ffn_kernel(ada_row, n_cast, *refs):
    h_ref, ada_ref, gpre_ref, w1_ref, w3_ref, w2_ref, gpost_ref = refs[:7]
    o_ref, act_ref = refs[7 + n_cast], refs[-1]
    _cast_blocks(refs[7:7 + n_cast], refs[8 + n_cast:8 + 2 * n_cast])
    tile = h_ref.shape[0]
    sub = min(FFN_SUBTILE, tile)
    consts = dict(shift=ada_ref[ada_row], scale=ada_ref[ada_row + 1], gate=ada_ref[ada_row + 2],
                  g_pre=gpre_ref[...], g_post=gpost_ref[...], w1=w1_ref, w3=w3_ref, w2=w2_ref)
    blocks = [_ffn_rows_steps(h_ref, o_ref, act_ref, r0, sub, consts) for r0 in range(0, tile, sub)]
    n_steps = 2 + D_FF // FFN_COLS + D_MODEL // FFN_COLS
    for step in range((len(blocks) - 1) * FFN_LAG + n_steps):
        for k, blk in enumerate(blocks):
            if 0 <= step - k * FFN_LAG < n_steps:
                next(blk)


def _ffn(h, ada, ada_row, g_pre, w1, w3, w2, g_post, casts=()):
    t = h.shape[0]
    tile = min(FFN_TILE, t)
    const = lambda i: (0, 0)
    cast_in, cast_out, cast_shapes = _cast_plan(casts, t // tile)
    out = pl.pallas_call(
        functools.partial(_ffn_kernel, ada_row, len(casts)),
        grid=(t // tile,),
        in_specs=[
            pl.BlockSpec((tile, D_MODEL), lambda i: (i, 0)),
            pl.BlockSpec((N_ADA, 1, D_MODEL), lambda i: (0, 0, 0)),
            pl.BlockSpec((1, D_MODEL), const),
            pl.BlockSpec((D_MODEL, D_FF), const, pipeline_mode=pl.Buffered(1)),
            pl.BlockSpec((D_MODEL, D_FF), const, pipeline_mode=pl.Buffered(1)),
            pl.BlockSpec((D_FF, D_MODEL), const, pipeline_mode=pl.Buffered(1)),
            pl.BlockSpec((1, D_MODEL), const),
        ] + cast_in,
        out_specs=[pl.BlockSpec((tile, D_MODEL), lambda i: (i, 0))] + cast_out,
        out_shape=[jax.ShapeDtypeStruct((t, D_MODEL), F32)] + cast_shapes,
        scratch_shapes=[pltpu.VMEM((tile, D_FF), BF16)],
        compiler_params=pltpu.CompilerParams(dimension_semantics=("arbitrary",),
                                             vmem_limit_bytes=VMEM_LIMIT),
        name="swiglu_half_step",
    )(h, ada, g_pre.reshape(1, D_MODEL), w1.astype(BF16), w3.astype(BF16), w2.astype(BF16),
      g_post.reshape(1, D_MODEL), *[arr for arr, _ in casts])
    return out[0], out[1:]


def _shift_rows(x, fill_row):
    row = lax.broadcasted_iota(jnp.int32, x.shape, 0)
    return jnp.where(row == 0, fill_row, pltpu.roll(x, 1, 0))


def _softplus(z):
    return jnp.maximum(z, 0.0) + jnp.log(1.0 + jnp.exp(-jnp.abs(z)))


def _bmm(a, b):
    return jnp.einsum("bmk,bkn->bmn", a.astype(BF16), b.astype(BF16), preferred_element_type=F32)


def _bmm_nt(a, b):
    return jnp.einsum("bmk,bnk->bmn", a.astype(BF16), b.astype(BF16), preferred_element_type=F32)


def _bmm_tn(a, b):
    return jnp.einsum("bkm,bkn->bmn", a.astype(BF16), b.astype(BF16), preferred_element_type=F32)


def _block_diag(x, left):
    x = x.astype(BF16)
    zero = jnp.zeros_like(x)
    return jnp.concatenate([jnp.where(left, x, zero), jnp.where(left, zero, x)], axis=1)


def _chunk_terms_steps(ph, rh, qh, kh, pt, rt, qb, kb, v, masks, res):
    left, strict, incl, diag_blk, eye, bd_mask = masks
    bd = lambda x: _block_diag(x, left)
    mm = lambda a, b: _bmm(a, bd(b))
    gram = _bmm_nt(jnp.concatenate([ph, rh], axis=1),
                   jnp.concatenate([bd(qh), bd(kh)], axis=1))
    a_pq = jnp.where(strict, gram[:, :CHUNK, :LANES], 0.0)
    a_rq = jnp.where(incl, gram[:, CHUNK:, :LANES], 0.0)
    a_pk = jnp.where(strict, gram[:, :CHUNK, LANES:], 0.0)
    a_rk = jnp.where(incl, gram[:, CHUNK:, LANES:], 0.0)
    yield
    a_d = jnp.where(diag_blk, a_pq, 0.0)
    a_o = a_pq - a_d
    x2 = mm(a_d, a_d)
    t_d = eye + a_d
    yield
    both = mm(jnp.concatenate([x2, t_d], axis=1), x2)
    x4, t_d = both[:, :CHUNK], t_d + both[:, CHUNK:]
    yield
    both = mm(jnp.concatenate([x4, t_d], axis=1), x4)
    x8, t_d = both[:, :CHUNK], t_d + both[:, CHUNK:]
    yield
    t_d = t_d + mm(t_d, x8)
    n1 = mm(t_d, a_o)
    yield
    n2 = mm(n1, n1)
    s = eye + n1
    s = s + mm(s, n2)
    yield
    t_inv = mm(s, t_d)
    both = mm(jnp.concatenate([a_pk, a_rk], axis=1), v)
    av, ark_v = both[:, :CHUNK], both[:, CHUNK:]
    yield
    wu = _bmm(t_inv, jnp.concatenate([bd(pt), bd(av)], axis=2))
    w, ub = wu[:, :, :LANES], wu[:, :, LANES:]
    ru = _bmm(a_rq, jnp.concatenate([bd(w), bd(ub)], axis=2))
    res["rw"] = rt + ru[:, :, :LANES]
    res["y0"] = ru[:, :, LANES:] + ark_v
    res["g_off"] = jnp.where(bd_mask, _bmm_tn(qb, w), 0.0)
    res["c_bd"] = jnp.where(bd_mask, _bmm_tn(jnp.concatenate([qb, kb], axis=1),
                                             jnp.concatenate([ub, v], axis=1)), 0.0)
    yield


def _seg_sum(z, ones_bd):
    half = ones_bd.shape[0]
    return jnp.concatenate([_dot(z[:, :half], ones_bd), _dot(z[:, half:], ones_bd)], axis=1)


def _mix_rows_steps(k, h_ref, o_ref, row0, rows, shared, c):
    n_chunks = rows // CHUNK
    cw, rwd = CONV_WIDTH, RWKV_WIDTH
    x = h_ref[row0:row0 + rows, :]
    u = (_rms(x) * c["g_pre"] * (1.0 + c["scale"]) + c["shift"]).astype(BF16)
    yield
    p_cols = []
    for j in range(6):
        p_cols.append(jnp.dot(u, c["w_in"][:, j * cw:(j + 1) * cw], preferred_element_type=F32))
        yield
    plo = jnp.dot(u, c["w_lo"][...], preferred_element_type=F32)
    c_pre, c_post, c_val = p_cols[0], p_cols[1], p_cols[2]
    rw_raw = jnp.concatenate(p_cols[3:], axis=1)
    cv = c_pre * c_val
    shared[k] = dict(prev_rw=rw_raw[rows - 1:, :], prev_lo=plo[rows - 1:, :],
                     prev_cv1=cv[rows - 1:, :], prev_cv2=cv[rows - 2:rows - 1, :])
    yield

    prev = shared[k - 1]
    conv_w = c["conv_w"]
    cv1 = _shift_rows(cv, prev["prev_cv1"])
    cv2 = _shift_rows(cv1, prev["prev_cv2"])
    y_conv = c_post * (conv_w[0:1, :] * cv2 + conv_w[1:2, :] * cv1 + conv_w[2:3, :] * cv)
    yield
    rw_mix = rw_raw + (_shift_rows(rw_raw, prev["prev_rw"]) - rw_raw) * c["mu"]
    lo_mix = plo + (_shift_rows(plo, prev["prev_lo"]) - plo) * c["mu_lo"]
    xr, xk, xv = rw_mix[:, :rwd], rw_mix[:, rwd:2 * rwd], rw_mix[:, 2 * rwd:]
    yield
    lane_lo = lax.broadcasted_iota(jnp.int32, lo_mix.shape, 1)
    lo_act = jnp.where(lane_lo < DECAY_LORA, jnp.tanh(lo_mix),
                       jnp.where(lane_lo < DECAY_LORA + AAA_LORA, lo_mix, jax.nn.sigmoid(lo_mix)))
    lup = _dot(lo_act, c["lora_up"][...])
    yield
    w_raw = -_softplus(-(c["w0"] + lup[:, :rwd])) - 0.5
    logw = -jnp.exp(w_raw)
    a = jax.nn.sigmoid(c["a0"] + lup[:, rwd:2 * rwd])
    g = lup[:, 2 * rwd:]
    yield
    ones_bd = c["ones"][...]
    kk = xk * c["k_k"]
    kk = kk / jnp.maximum(jnp.sqrt(_seg_sum(kk * kk, ones_bd)), 1e-12)
    kmod = xk * (1.0 + (a - 1.0) * c["k_a"])
    bonus = _seg_sum(xr * kmod * c["r_k"], ones_bd) * xv
    pv = -kk
    qv = kk * a
    yield
    logw_hi = logw.astype(BF16)
    logw_lo = (logw - logw_hi.astype(F32)).astype(BF16)
    gc = (jnp.dot(c["dmat"], logw_hi, preferred_element_type=F32)
          + jnp.dot(c["dmat"], logw_lo, preferred_element_type=F32))
    e_fwd = jnp.exp(gc)
    e_bwd = jnp.exp(-gc)
    e_prev = jnp.exp(gc - logw)
    yield
    per_chunk = lambda z: z.reshape(n_chunks, CHUNK, rwd)
    gc3, lw3 = per_chunk(gc), per_chunk(logw)
    g_mid = lw3[:, 0:1, :] - gc3[:, 0:1, :]
    g_end = gc3[:, CHUNK - 1:CHUNK, :]
    e_mid = jnp.exp(g_mid)
    e_end = jnp.exp(g_end)
    decay_end = jnp.exp(g_mid + g_end)
    scale_chunk = lambda z, f: (per_chunk(z) * f).reshape(rows, rwd)
    ph = pv * e_prev
    rh = xr * e_fwd
    qh = qv * e_bwd
    kh = kmod * e_bwd
    yield
    pt = scale_chunk(ph, e_mid)
    rt = scale_chunk(rh, e_mid)
    qb = scale_chunk(qh, e_end)
    kb = scale_chunk(kh, e_end)

    def batch(z):
        return jnp.stack([z[ci * CHUNK:(ci + 1) * CHUNK, j * LANES:(j + 1) * LANES]
                          for ci in range(n_chunks) for j in range(N_PAIRS)], axis=0)

    terms = {}
    chunk_steps = _chunk_terms_steps(batch(ph), batch(rh), batch(qh), batch(kh), batch(pt),
                                     batch(rt), batch(qb), batch(kb), batch(xv), c["masks"], terms)
    yield

    for _ in chunk_steps:
        yield

    m = shared[k - 1]["state"]
    o_rows = []
    for ci in range(n_chunks):
        b = slice(ci * N_PAIRS, (ci + 1) * N_PAIRS)
        o_c = _bmm(terms["rw"][b], m) + terms["y0"][b]
        o_rows.append(jnp.concatenate([o_c[j] for j in range(N_PAIRS)], axis=1))
        dec = jnp.stack([jnp.where(c["eye2"], decay_end[ci, :, j * LANES:(j + 1) * LANES], 0.0)
                         for j in range(N_PAIRS)], axis=0)
        m = jnp.sum(dec, axis=2, keepdims=True) * m + _bmm(terms["g_off"][b], m) + terms["c_bd"][b]
        if ci == n_chunks - 1:
            shared[k]["state"] = m
        yield
    o = jnp.concatenate(o_rows, axis=0)
    inv_n = 1.0 / RWKV_HEAD
    o_c = o - _seg_sum(o, ones_bd) * inv_n
    yield
    o_var = _seg_sum(o_c * o_c, ones_bd) * inv_n
    o_n = o_c * lax.rsqrt(o_var + GN_EPS) * c["ln_w"] + c["ln_b"]
    y_rwkv = (o_n + bonus) * g
    y_in = jnp.concatenate([y_conv, y_rwkv], axis=1).astype(BF16)
    yield
    yield
    yield

    y_cols = []
    for j in range(D_MODEL // MXU_DIM):
        y_cols.append(jnp.dot(y_in, c["w_out"][:, j * MXU_DIM:(j + 1) * MXU_DIM],
                              preferred_element_type=F32))
        yield
    y = jnp.concatenate(y_cols, axis=1)
    o_ref[row0:row0 + rows, :] = x + c["gate"] * (_rms(y) * c["g_post"])
    yield
    yield
    yield
    yield


def _mixer_kernel(n_cast, *refs):
    (h_ref, ada_ref, gpre_ref, win_ref, wlo_ref, mu_ref, mulo_ref, convw_ref, w0_ref, a0_ref,
     loraup_ref, kk_ref, ka_ref, rk_ref, lnw_ref, lnb_ref, ones_ref, wout_ref,
     gpost_ref) = refs[:MIX_INPUTS]
    o_ref = refs[MIX_INPUTS + n_cast]
    prev_rw_ref, prev_lo_ref, prev_cv_ref, state_ref = refs[-4:]
    _cast_blocks(refs[MIX_INPUTS:MIX_INPUTS + n_cast],
                 refs[MIX_INPUTS + n_cast + 1:MIX_INPUTS + 2 * n_cast + 1])
    tile = h_ref.shape[0]
    sub = min(MIX_SUBTILE, tile)
    assert sub // CHUNK == MIX_STEPS // 2

    @pl.when(pl.program_id(0) == 0)
    def _():
        prev_rw_ref[...] = jnp.zeros_like(prev_rw_ref)
        prev_lo_ref[...] = jnp.zeros_like(prev_lo_ref)
        prev_cv_ref[...] = jnp.zeros_like(prev_cv_ref)
        state_ref[...] = jnp.zeros_like(state_ref)

    ti = lax.broadcasted_iota(jnp.int32, (sub, sub), 0)
    si = lax.broadcasted_iota(jnp.int32, (sub, sub), 1)
    tl, sl = ti % CHUNK, si % CHUNK
    mid = CHUNK // 2 - 1
    dmat = jnp.where((ti // CHUNK) == (si // CHUNK),
                     (sl <= tl).astype(F32) - (sl <= mid).astype(F32), 0.0).astype(BF16)
    r_i = lax.broadcasted_iota(jnp.int32, (CHUNK, LANES), 0)
    l_i = lax.broadcasted_iota(jnp.int32, (CHUNK, LANES), 1)
    col = l_i % RWKV_HEAD
    r2 = lax.broadcasted_iota(jnp.int32, (LANES, LANES), 0)
    c2 = lax.broadcasted_iota(jnp.int32, (LANES, LANES), 1)
    masks = (l_i < RWKV_HEAD, r_i > col, r_i >= col, (r_i // SUB) == (col // SUB),
             (r_i == col).astype(F32), (r2 // RWKV_HEAD) == (c2 // RWKV_HEAD))
    consts = dict(
        shift=ada_ref[3], scale=ada_ref[4], gate=ada_ref[5], g_pre=gpre_ref[...],
        w_in=win_ref, w_lo=wlo_ref, mu=mu_ref[...], mu_lo=mulo_ref[...], conv_w=convw_ref[...],
        w0=w0_ref[...], a0=a0_ref[...], lora_up=loraup_ref, k_k=kk_ref[...], k_a=ka_ref[...],
        r_k=rk_ref[...], ln_w=lnw_ref[...], ln_b=lnb_ref[...], ones=ones_ref, w_out=wout_ref,
        g_post=gpost_ref[...], dmat=dmat, masks=masks, eye2=r2 == c2)

    n_sub = tile // sub
    shared = {-1: dict(prev_rw=prev_rw_ref[...], prev_lo=prev_lo_ref[...],
                       prev_cv1=prev_cv_ref[0:1, :], prev_cv2=prev_cv_ref[1:2, :],
                       state=state_ref[...])}
    blocks = [_mix_rows_steps(k, h_ref, o_ref, k * sub, sub, shared, consts) for k in range(n_sub)]
    for step in range((n_sub - 1) * MIX_LAG + MIX_PHASES * MIX_STEPS):
        for k, blk in enumerate(blocks):
            if 0 <= step - k * MIX_LAG < MIX_PHASES * MIX_STEPS:
                next(blk)
    final = shared[n_sub - 1]
    prev_rw_ref[...] = final["prev_rw"]
    prev_lo_ref[...] = final["prev_lo"]
    prev_cv_ref[0:1, :] = final["prev_cv1"]
    prev_cv_ref[1:2, :] = final["prev_cv2"]
    state_ref[...] = final["state"]


def _mixer(h, ada, g_pre, w_main, w_in, conv_w, mu_shift, w0, w_up, a0, a_up, g_up, k_k, k_a, r_k,
           ln_x_w, ln_x_b, w_out, g_post, casts=()):
    t = h.shape[0]
    tile = min(MIX_TILE, t)
    n_main = MIX_MAIN_COLS
    w_lo = jnp.pad(w_in[:, n_main:], ((0, 0), (0, LORA_PAD - LORA_COLS))).astype(BF16)
    mu_main = mu_shift[:3 * RWKV_WIDTH].reshape(1, -1)
    mu_lo = jnp.pad(mu_shift[3 * RWKV_WIDTH:], (0, LORA_PAD - LORA_COLS)).reshape(1, -1)
    lora_up = jnp.zeros((LORA_PAD, 3 * RWKV_WIDTH), F32)
    lora_up = lora_up.at[:DECAY_LORA, :RWKV_WIDTH].set(w_up)
    lora_up = lora_up.at[DECAY_LORA:DECAY_LORA + AAA_LORA, RWKV_WIDTH:2 * RWKV_WIDTH].set(a_up)
    lora_up = lora_up.at[DECAY_LORA + AAA_LORA:LORA_COLS, 2 * RWKV_WIDTH:].set(g_up)
    head_id = jnp.arange(MXU_DIM) // RWKV_HEAD
    ones_bd = (head_id[:, None] == head_id[None, :]).astype(BF16)
    row = lambda z: z.reshape(1, -1)
    const = lambda i: (0, 0)
    full = lambda shape: pl.BlockSpec(shape, const)
    cast_in, cast_out, cast_shapes = _cast_plan(casts, t // tile)
    out = pl.pallas_call(
        functools.partial(_mixer_kernel, len(casts)),
        grid=(t // tile,),
        in_specs=[
            pl.BlockSpec((tile, D_MODEL), lambda i: (i, 0)),
            pl.BlockSpec((N_ADA, 1, D_MODEL), lambda i: (0, 0, 0)),
            full((1, D_MODEL)),
            full((D_MODEL, n_main)),
            full((D_MODEL, LORA_PAD)),
            full((1, 3 * RWKV_WIDTH)),
            full((1, LORA_PAD)),
            full((CONV_K, CONV_WIDTH)),
            full((1, RWKV_WIDTH)),
            full((1, RWKV_WIDTH)),
            full((LORA_PAD, 3 * RWKV_WIDTH)),
            full((1, RWKV_WIDTH)),
            full((1, RWKV_WIDTH)),
            full((1, RWKV_WIDTH)),
            full((1, RWKV_WIDTH)),
            full((1, RWKV_WIDTH)),
            full((MXU_DIM, MXU_DIM)),
            full((D_MODEL, D_MODEL)),
            full((1, D_MODEL)),
        ] + cast_in,
        out_specs=[pl.BlockSpec((tile, D_MODEL), lambda i: (i, 0))] + cast_out,
        out_shape=[jax.ShapeDtypeStruct((t, D_MODEL), F32)] + cast_shapes,
        scratch_shapes=[
            pltpu.VMEM((1, 3 * RWKV_WIDTH), F32),
            pltpu.VMEM((1, LORA_PAD), F32),
            pltpu.VMEM((2, CONV_WIDTH), F32),
            pltpu.VMEM((N_PAIRS, LANES, LANES), F32),
        ],
        compiler_params=pltpu.CompilerParams(dimension_semantics=("arbitrary",),
                                             vmem_limit_bytes=VMEM_LIMIT),
        name="token_mixing",
    )(h, ada, row(g_pre), w_main, w_lo, mu_main, mu_lo, conv_w, row(w0), row(a0),
      lora_up.astype(BF16), row(k_k), row(k_a), row(r_k), row(ln_x_w), row(ln_x_b), ones_bd,
      w_out, row(g_post), *[arr for arr, _ in casts])
    return out[0], out[1:]


def kernel(x, c, w_ada, b_ada, ffn1_g_pre, ffn1_w1, ffn1_w3, ffn1_w2, ffn1_g_post, mix_g_pre, w_in, conv_w, mu_shift, w0, w_up, a0, a_up, g_up, k_k, k_a, r_k, ln_x_w, ln_x_b, w_out, mix_g_post, ffn2_g_pre, ffn2_w1, ffn2_w3, ffn2_w2, ffn2_g_post):
    bsz, t, _ = x.shape
    outs = []
    for b in range(bsz):
        h = x[b]
        for l in range(w_ada.shape[0]):
            ada = _ada(c[b], w_ada[l], b_ada[l])
            h, (w_main, w_out_b) = _ffn(
                h, ada, 0, ffn1_g_pre[l], ffn1_w1[l], ffn1_w3[l], ffn1_w2[l], ffn1_g_post[l],
                casts=[(w_in[l], MIX_MAIN_COLS), (w_out[l], D_MODEL)])
            h, (w1_b, w3_b, w2_b) = _mixer(
                h, ada, mix_g_pre[l], w_main, w_in[l], conv_w[l], mu_shift[l], w0[l], w_up[l],
                a0[l], a_up[l], g_up[l], k_k[l], k_a[l], r_k[l], ln_x_w[l], ln_x_b[l],
                w_out_b, mix_g_post[l],
                casts=[(ffn2_w1[l], D_FF), (ffn2_w3[l], D_FF), (ffn2_w2[l], D_MODEL)])
            h, _ = _ffn(h, ada, 6, ffn2_g_pre[l], w1_b, w3_b, w2_b, ffn2_g_post[l])
        outs.append(h)
    return jnp.stack(outs, axis=0)
```

```python
import functools

import jax
import jax.numpy as jnp
from jax import lax
from jax.experimental import pallas as pl
from jax.experimental.pallas import tpu as pltpu

F32 = jnp.float32
BF16 = jnp.bfloat16

D_MODEL = 1024
D_FF = 2816
CONV_WIDTH = 512
CONV_K = 3
RWKV_WIDTH = 512
RWKV_HEAD = 64
RWKV_HEADS = RWKV_WIDTH // RWKV_HEAD
DECAY_LORA = 32
AAA_LORA = 32
GATE_LORA = 96
LORA_COLS = DECAY_LORA + AAA_LORA + GATE_LORA
N_ADA = 9
MACARON_W = 0.5
NORM_EPS = 1e-6
GN_EPS = 64e-5

LANES = 128
SUBLANES = 8
MXU_DIM = 256
LORA_PAD = MXU_DIM
BF16_ROWS = 16
MIX_MAIN_COLS = 3 * CONV_WIDTH + 3 * RWKV_WIDTH
MIX_INPUTS = 19
HEADS_PER_PAIR = LANES // RWKV_HEAD
N_PAIRS = RWKV_WIDTH // LANES
CHUNK = 64
SUB = 8
MIX_TILE = 512
MIX_SUBTILE = 256
MIX_PHASES = 5
MIX_LAG = 8
MIX_STEPS = 8
FFN_TILE = 1024
FFN_SUBTILE = 256
FFN_LAG = 4
FFN_COLS = MXU_DIM
ADA_ROWS = 128
VMEM_LIMIT = 56 * 1024 * 1024


def _rms(x):
    return x * lax.rsqrt(jnp.mean(x * x, axis=-1, keepdims=True) + NORM_EPS)


def _dot(a, b):
    return jnp.dot(a.astype(BF16), b.astype(BF16), preferred_element_type=F32)


def _ada_kernel(c_ref, w_ref, b_ref, o_ref):
    @pl.when(pl.program_id(0) == 0)
    def _():
        o_ref[...] = b_ref[...]

    c = c_ref[...]
    cond = c * jax.nn.sigmoid(c)
    part = jnp.sum(w_ref[...] * cond, axis=0, keepdims=True)
    for k in range(N_ADA):
        o_ref[k] += part[:, k * D_MODEL:(k + 1) * D_MODEL]


def _ada(c, w_ada, b_ada):
    full = pl.BlockSpec((N_ADA, 1, D_MODEL), lambda j: (0, 0, 0))
    return pl.pallas_call(
        _ada_kernel,
        grid=(D_MODEL // ADA_ROWS,),
        in_specs=[
            pl.BlockSpec((ADA_ROWS, 1), lambda j: (j, 0)),
            pl.BlockSpec((ADA_ROWS, N_ADA * D_MODEL), lambda j: (j, 0)),
            full,
        ],
        out_specs=full,
        out_shape=jax.ShapeDtypeStruct((N_ADA, 1, D_MODEL), F32),
        compiler_params=pltpu.CompilerParams(dimension_semantics=("arbitrary",)),
        name="ada_proj",
    )(c.reshape(D_MODEL, 1), w_ada, b_ada.reshape(N_ADA, 1, D_MODEL))


def _ffn_rows_steps(h_ref, o_ref, act_ref, row0, rows, c):
    x = h_ref[row0:row0 + rows, :]
    u = (_rms(x) * c["g_pre"] * (1.0 + c["scale"]) + c["shift"]).astype(BF16)
    yield
    for j in range(D_FF // FFN_COLS):
        cols = slice(j * FFN_COLS, (j + 1) * FFN_COLS)
        a = jnp.dot(u, c["w1"][:, cols], preferred_element_type=F32)
        b = jnp.dot(u, c["w3"][:, cols], preferred_element_type=F32)
        act_ref[row0:row0 + rows, cols] = (a * jax.nn.sigmoid(a) * b).astype(BF16)
        yield
    act = act_ref[row0:row0 + rows, :]
    y_cols = []
    for j in range(D_MODEL // FFN_COLS):
        cols = slice(j * FFN_COLS, (j + 1) * FFN_COLS)
        y_cols.append(jnp.dot(act, c["w2"][:, cols], preferred_element_type=F32))
        yield
    y = jnp.concatenate(y_cols, axis=1)
    o_ref[row0:row0 + rows, :] = x + MACARON_W * c["gate"] * (_rms(y) * c["g_post"])
    yield


def _cast_plan(casts, n_steps):
    in_specs, out_specs, out_shapes = [], [], []
    for arr, cols in casts:
        rows = arr.shape[0]
        n_blk = n_steps
        while rows % n_blk or (rows // n_blk) % BF16_ROWS:
            n_blk //= 2
        idx = lambda i, every=n_steps // n_blk: (i // every, 0)
        in_specs.append(pl.BlockSpec((rows // n_blk, arr.shape[1]), idx))
        out_specs.append(pl.BlockSpec((rows // n_blk, cols), idx))
        out_shapes.append(jax.ShapeDtypeStruct((rows, cols), BF16))
    return in_specs, out_specs, out_shapes


def _cast_blocks(src_refs, dst_refs):
    for src, dst in zip(src_refs, dst_refs):
        dst[...] = src[:, :dst.shape[1]].astype(BF16)


def _ffn_kernel(ada_row, n_cast, *refs):
    h_ref, ada_ref, gpre_ref, w1_ref, w3_ref, w2_ref, gpost_ref = refs[:7]
    o_ref, act_ref = refs[7 + n_cast], refs[-1]
    _cast_blocks(refs[7:7 + n_cast], refs[8 + n_cast:8 + 2 * n_cast])
    tile = h_ref.shape[0]
    sub = min(FFN_SUBTILE, tile)
    consts = dict(shift=ada_ref[ada_row], scale=ada_ref[ada_row + 1], gate=ada_ref[ada_row + 2],
                  g_pre=gpre_ref[...], g_post=gpost_ref[...], w1=w1_ref, w3=w3_ref, w2=w2_ref)
    blocks = [_ffn_rows_steps(h_ref, o_ref, act_ref, r0, sub, consts) for r0 in range(0, tile, sub)]
    n_steps = 2 + D_FF // FFN_COLS + D_MODEL // FFN_COLS
    for step in range((len(blocks) - 1) * FFN_LAG + n_steps):
        for k, blk in enumerate(blocks):
            if 0 <= step - k * FFN_LAG < n_steps:
                next(blk)


def _ffn(h, ada, ada_row, g_pre, w1, w3, w2, g_post, casts=()):
    t = h.shape[0]
    tile = min(FFN_TILE, t)
    const = lambda i: (0, 0)
    cast_in, cast_out, cast_shapes = _cast_plan(casts, t // tile)
    out = pl.pallas_call(
        functools.partial(_ffn_kernel, ada_row, len(casts)),
        grid=(t // tile,),
        in_specs=[
            pl.BlockSpec((tile, D_MODEL), lambda i: (i, 0)),
            pl.BlockSpec((N_ADA, 1, D_MODEL), lambda i: (0, 0, 0)),
            pl.BlockSpec((1, D_MODEL), const),
            pl.BlockSpec((D_MODEL, D_FF), const, pipeline_mode=pl.Buffered(1)),
            pl.BlockSpec((D_MODEL, D_FF), const, pipeline_mode=pl.Buffered(1)),
            pl.BlockSpec((D_FF, D_MODEL), const, pipeline_mode=pl.Buffered(1)),
            pl.BlockSpec((1, D_MODEL), const),
        ] + cast_in,
        out_specs=[pl.BlockSpec((tile, D_MODEL), lambda i: (i, 0))] + cast_out,
        out_shape=[jax.ShapeDtypeStruct((t, D_MODEL), F32)] + cast_shapes,
        scratch_shapes=[pltpu.VMEM((tile, D_FF), BF16)],
        compiler_params=pltpu.CompilerParams(dimension_semantics=("arbitrary",),
                                             vmem_limit_bytes=VMEM_LIMIT),
        name="swiglu_half_step",
    )(h, ada, g_pre.reshape(1, D_MODEL), w1.astype(BF16), w3.astype(BF16), w2.astype(BF16),
      g_post.reshape(1, D_MODEL), *[arr for arr, _ in casts])
    return out[0], out[1:]


def _shift_rows(x, fill_row):
    row = lax.broadcasted_iota(jnp.int32, x.shape, 0)
    return jnp.where(row == 0, fill_row, pltpu.roll(x, 1, 0))


def _softplus(z):
    return jnp.maximum(z, 0.0) + jnp.log(1.0 + jnp.exp(-jnp.abs(z)))


def _bmm(a, b):
    return jnp.einsum("bmk,bkn->bmn", a.astype(BF16), b.astype(BF16), preferred_element_type=F32)


def _bmm_nt(a, b):
    return jnp.einsum("bmk,bnk->bmn", a.astype(BF16), b.astype(BF16), preferred_element_type=F32)


def _bmm_tn(a, b):
    return jnp.einsum("bkm,bkn->bmn", a.astype(BF16), b.astype(BF16), preferred_element_type=F32)


def _block_diag(x, left):
    x = x.astype(BF16)
    zero = jnp.zeros_like(x)
    return jnp.concatenate([jnp.where(left, x, zero), jnp.where(left, zero, x)], axis=1)


def _chunk_terms_steps(ph, rh, qh, kh, pt, rt, qb, kb, v, masks, res):
    left, strict, incl, diag_blk, eye, bd_mask = masks
    bd = lambda x: _block_diag(x, left)
    mm = lambda a, b: _bmm(a, bd(b))
    gram = _bmm_nt(jnp.concatenate([ph, rh], axis=1),
                   jnp.concatenate([bd(qh), bd(kh)], axis=1))
    a_pq = jnp.where(strict, gram[:, :CHUNK, :LANES], 0.0)
    a_rq = jnp.where(incl, gram[:, CHUNK:, :LANES], 0.0)
    a_pk = jnp.where(strict, gram[:, :CHUNK, LANES:], 0.0)
    a_rk = jnp.where(incl, gram[:, CHUNK:, LANES:], 0.0)
    yield
    a_d = jnp.where(diag_blk, a_pq, 0.0)
    a_o = a_pq - a_d
    x2 = mm(a_d, a_d)
    t_d = eye + a_d
    yield
    both = mm(jnp.concatenate([x2, t_d], axis=1), x2)
    x4, t_d = both[:, :CHUNK], t_d + both[:, CHUNK:]
    yield
    t_d = t_d + mm(t_d, x4)
    n1 = mm(t_d, a_o)
    yield
    n2 = mm(n1, n1)
    s = eye + n1
    yield
    both = mm(jnp.concatenate([n2, s], axis=1), n2)
    n4, s = both[:, :CHUNK], s + both[:, CHUNK:]
    yield
    s = s + mm(s, n4)
    t_inv = mm(s, t_d)
    both = mm(jnp.concatenate([a_pk, a_rk], axis=1), v)
    av, ark_v = both[:, :CHUNK], both[:, CHUNK:]
    yield
    wu = _bmm(t_inv, jnp.concatenate([bd(pt), bd(av)], axis=2))
    w, ub = wu[:, :, :LANES], wu[:, :, LANES:]
    ru = _bmm(a_rq, jnp.concatenate([bd(w), bd(ub)], axis=2))
    res["rw"] = rt + ru[:, :, :LANES]
    res["y0"] = ru[:, :, LANES:] + ark_v
    res["g_off"] = jnp.where(bd_mask, _bmm_tn(qb, w), 0.0)
    res["c_bd"] = jnp.where(bd_mask, _bmm_tn(jnp.concatenate([qb, kb], axis=1),
                                             jnp.concatenate([ub, v], axis=1)), 0.0)
    yield


def _seg_sum(z, ones_bd):
    half = ones_bd.shape[0]
    return jnp.concatenate([_dot(z[:, :half], ones_bd), _dot(z[:, half:], ones_bd)], axis=1)


def _mix_rows_steps(k, h_ref, o_ref, row0, rows, shared, c):
    n_chunks = rows // CHUNK
    cw, rwd = CONV_WIDTH, RWKV_WIDTH
    x = h_ref[row0:row0 + rows, :]
    u = (_rms(x) * c["g_pre"] * (1.0 + c["scale"]) + c["shift"]).astype(BF16)
    yield
    p_cols = []
    for j in range(6):
        p_cols.append(jnp.dot(u, c["w_in"][:, j * cw:(j + 1) * cw], preferred_element_type=F32))
        yield
    plo = jnp.dot(u, c["w_lo"][...], preferred_element_type=F32)
    c_pre, c_post, c_val = p_cols[0], p_cols[1], p_cols[2]
    rw_raw = jnp.concatenate(p_cols[3:], axis=1)
    cv = c_pre * c_val
    shared[k] = dict(prev_rw=rw_raw[rows - 1:, :], prev_lo=plo[rows - 1:, :],
                     prev_cv1=cv[rows - 1:, :], prev_cv2=cv[rows - 2:rows - 1, :])
    yield

    prev = shared[k - 1]
    conv_w = c["conv_w"]
    cv1 = _shift_rows(cv, prev["prev_cv1"])
    cv2 = _shift_rows(cv1, prev["prev_cv2"])
    y_conv = c_post * (conv_w[0:1, :] * cv2 + conv_w[1:2, :] * cv1 + conv_w[2:3, :] * cv)
    yield
    rw_mix = rw_raw + (_shift_rows(rw_raw, prev["prev_rw"]) - rw_raw) * c["mu"]
    lo_mix = plo + (_shift_rows(plo, prev["prev_lo"]) - plo) * c["mu_lo"]
    xr, xk, xv = rw_mix[:, :rwd], rw_mix[:, rwd:2 * rwd], rw_mix[:, 2 * rwd:]
    yield
    lane_lo = lax.broadcasted_iota(jnp.int32, lo_mix.shape, 1)
    lo_act = jnp.where(lane_lo < DECAY_LORA, jnp.tanh(lo_mix),
                       jnp.where(lane_lo < DECAY_LORA + AAA_LORA, lo_mix, jax.nn.sigmoid(lo_mix)))
    lup = _dot(lo_act, c["lora_up"][...])
    yield
    w_raw = -_softplus(-(c["w0"] + lup[:, :rwd])) - 0.5
    logw = -jnp.exp(w_raw)
    a = jax.nn.sigmoid(c["a0"] + lup[:, rwd:2 * rwd])
    g = lup[:, 2 * rwd:]
    yield
    ones_bd = c["ones"][...]
    kk = xk * c["k_k"]
    kk = kk / jnp.maximum(jnp.sqrt(_seg_sum(kk * kk, ones_bd)), 1e-12)
    kmod = xk * (1.0 + (a - 1.0) * c["k_a"])
    bonus = _seg_sum(xr * kmod * c["r_k"], ones_bd) * xv
    pv = -kk
    qv = kk * a
    yield
    logw_hi = logw.astype(BF16)
    logw_lo = (logw - logw_hi.astype(F32)).astype(BF16)
    gc = (jnp.dot(c["dmat"], logw_hi, preferred_element_type=F32)
          + jnp.dot(c["dmat"], logw_lo, preferred_element_type=F32))
    e_fwd = jnp.exp(gc)
    e_bwd = jnp.exp(-gc)
    e_prev = jnp.exp(gc - logw)
    yield
    per_chunk = lambda z: z.reshape(n_chunks, CHUNK, rwd)
    gc3, lw3 = per_chunk(gc), per_chunk(logw)
    g_mid = lw3[:, 0:1, :] - gc3[:, 0:1, :]
    g_end = gc3[:, CHUNK - 1:CHUNK, :]
    e_mid = jnp.exp(g_mid)
    e_end = jnp.exp(g_end)
    decay_end = jnp.exp(g_mid + g_end)
    scale_chunk = lambda z, f: (per_chunk(z) * f).reshape(rows, rwd)
    ph = pv * e_prev
    rh = xr * e_fwd
    qh = qv * e_bwd
    kh = kmod * e_bwd
    yield
    pt = scale_chunk(ph, e_mid)
    rt = scale_chunk(rh, e_mid)
    qb = scale_chunk(qh, e_end)
    kb = scale_chunk(kh, e_end)

    def batch(z):
        return jnp.stack([z[ci * CHUNK:(ci + 1) * CHUNK, j * LANES:(j + 1) * LANES]
                          for ci in range(n_chunks) for j in range(N_PAIRS)], axis=0)

    terms = {}
    chunk_steps = _chunk_terms_steps(batch(ph), batch(rh), batch(qh), batch(kh), batch(pt),
                                     batch(rt), batch(qb), batch(kb), batch(xv), c["masks"], terms)
    yield

    for _ in chunk_steps:
        yield

    m = shared[k - 1]["state"]
    o_rows = []
    for ci in range(n_chunks):
        b = slice(ci * N_PAIRS, (ci + 1) * N_PAIRS)
        o_c = _bmm(terms["rw"][b], m) + terms["y0"][b]
        o_rows.append(jnp.concatenate([o_c[j] for j in range(N_PAIRS)], axis=1))
        dec = jnp.stack([jnp.where(c["eye2"], decay_end[ci, :, j * LANES:(j + 1) * LANES], 0.0)
                         for j in range(N_PAIRS)], axis=0)
        m = jnp.sum(dec, axis=2, keepdims=True) * m + _bmm(terms["g_off"][b], m) + terms["c_bd"][b]
        if ci == n_chunks - 1:
            shared[k]["state"] = m
        yield
    o = jnp.concatenate(o_rows, axis=0)
    inv_n = 1.0 / RWKV_HEAD
    o_c = o - _seg_sum(o, ones_bd) * inv_n
    yield
    o_var = _seg_sum(o_c * o_c, ones_bd) * inv_n
    o_n = o_c * lax.rsqrt(o_var + GN_EPS) * c["ln_w"] + c["ln_b"]
    y_rwkv = (o_n + bonus) * g
    y_in = jnp.concatenate([y_conv, y_rwkv], axis=1).astype(BF16)
    yield
    yield
    yield

    y_cols = []
    for j in range(D_MODEL // MXU_DIM):
        y_cols.append(jnp.dot(y_in, c["w_out"][:, j * MXU_DIM:(j + 1) * MXU_DIM],
                              preferred_element_type=F32))
        yield
    y = jnp.concatenate(y_cols, axis=1)
    o_ref[row0:row0 + rows, :] = x + c["gate"] * (_rms(y) * c["g_post"])
    yield
    yield
    yield
    yield


def _mixer_kernel(n_cast, *refs):
    (h_ref, ada_ref, gpre_ref, win_ref, wlo_ref, mu_ref, mulo_ref, convw_ref, w0_ref, a0_ref,
     loraup_ref, kk_ref, ka_ref, rk_ref, lnw_ref, lnb_ref, ones_ref, wout_ref,
     gpost_ref) = refs[:MIX_INPUTS]
    o_ref = refs[MIX_INPUTS + n_cast]
    prev_rw_ref, prev_lo_ref, prev_cv_ref, state_ref = refs[-4:]
    _cast_blocks(refs[MIX_INPUTS:MIX_INPUTS + n_cast],
                 refs[MIX_INPUTS + n_cast + 1:MIX_INPUTS + 2 * n_cast + 1])
    tile = h_ref.shape[0]
    sub = min(MIX_SUBTILE, tile)
    assert sub // CHUNK == MIX_STEPS // 2

    @pl.when(pl.program_id(0) == 0)
    def _():
        prev_rw_ref[...] = jnp.zeros_like(prev_rw_ref)
        prev_lo_ref[...] = jnp.zeros_like(prev_lo_ref)
        prev_cv_ref[...] = jnp.zeros_like(prev_cv_ref)
        state_ref[...] = jnp.zeros_like(state_ref)

    ti = lax.broadcasted_iota(jnp.int32, (sub, sub), 0)
    si = lax.broadcasted_iota(jnp.int32, (sub, sub), 1)
    tl, sl = ti % CHUNK, si % CHUNK
    mid = CHUNK // 2 - 1
    dmat = jnp.where((ti // CHUNK) == (si // CHUNK),
                     (sl <= tl).astype(F32) - (sl <= mid).astype(F32), 0.0).astype(BF16)
    r_i = lax.broadcasted_iota(jnp.int32, (CHUNK, LANES), 0)
    l_i = lax.broadcasted_iota(jnp.int32, (CHUNK, LANES), 1)
    col = l_i % RWKV_HEAD
    r2 = lax.broadcasted_iota(jnp.int32, (LANES, LANES), 0)
    c2 = lax.broadcasted_iota(jnp.int32, (LANES, LANES), 1)
    masks = (l_i < RWKV_HEAD, r_i > col, r_i >= col, (r_i // SUB) == (col // SUB),
             (r_i == col).astype(F32), (r2 // RWKV_HEAD) == (c2 // RWKV_HEAD))
    consts = dict(
        shift=ada_ref[3], scale=ada_ref[4], gate=ada_ref[5], g_pre=gpre_ref[...],
        w_in=win_ref, w_lo=wlo_ref, mu=mu_ref[...], mu_lo=mulo_ref[...], conv_w=convw_ref[...],
        w0=w0_ref[...], a0=a0_ref[...], lora_up=loraup_ref, k_k=kk_ref[...], k_a=ka_ref[...],
        r_k=rk_ref[...], ln_w=lnw_ref[...], ln_b=lnb_ref[...], ones=ones_ref, w_out=wout_ref,
        g_post=gpost_ref[...], dmat=dmat, masks=masks, eye2=r2 == c2)

    n_sub = tile // sub
    shared = {-1: dict(prev_rw=prev_rw_ref[...], prev_lo=prev_lo_ref[...],
                       prev_cv1=prev_cv_ref[0:1, :], prev_cv2=prev_cv_ref[1:2, :],
                       state=state_ref[...])}
    blocks = [_mix_rows_steps(k, h_ref, o_ref, k * sub, sub, shared, consts) for k in range(n_sub)]
    for step in range((n_sub - 1) * MIX_LAG + MIX_PHASES * MIX_STEPS):
        for k, blk in enumerate(blocks):
            if 0 <= step - k * MIX_LAG < MIX_PHASES * MIX_STEPS:
                next(blk)
    final = shared[n_sub - 1]
    prev_rw_ref[...] = final["prev_rw"]
    prev_lo_ref[...] = final["prev_lo"]
    prev_cv_ref[0:1, :] = final["prev_cv1"]
    prev_cv_ref[1:2, :] = final["prev_cv2"]
    state_ref[...] = final["state"]


def _mixer(h, ada, g_pre, w_main, w_in, conv_w, mu_shift, w0, w_up, a0, a_up, g_up, k_k, k_a, r_k,
           ln_x_w, ln_x_b, w_out, g_post, casts=()):
    t = h.shape[0]
    tile = min(MIX_TILE, t)
    n_main = MIX_MAIN_COLS
    w_lo = jnp.pad(w_in[:, n_main:], ((0, 0), (0, LORA_PAD - LORA_COLS))).astype(BF16)
    mu_main = mu_shift[:3 * RWKV_WIDTH].reshape(1, -1)
    mu_lo = jnp.pad(mu_shift[3 * RWKV_WIDTH:], (0, LORA_PAD - LORA_COLS)).reshape(1, -1)
    lora_up = jnp.zeros((LORA_PAD, 3 * RWKV_WIDTH), F32)
    lora_up = lora_up.at[:DECAY_LORA, :RWKV_WIDTH].set(w_up)
    lora_up = lora_up.at[DECAY_LORA:DECAY_LORA + AAA_LORA, RWKV_WIDTH:2 * RWKV_WIDTH].set(a_up)
    lora_up = lora_up.at[DECAY_LORA + AAA_LORA:LORA_COLS, 2 * RWKV_WIDTH:].set(g_up)
    head_id = jnp.arange(MXU_DIM) // RWKV_HEAD
    ones_bd = (head_id[:, None] == head_id[None, :]).astype(BF16)
    row = lambda z: z.reshape(1, -1)
    const = lambda i: (0, 0)
    full = lambda shape: pl.BlockSpec(shape, const)
    cast_in, cast_out, cast_shapes = _cast_plan(casts, t // tile)
    out = pl.pallas_call(
        functools.partial(_mixer_kernel, len(casts)),
        grid=(t // tile,),
        in_specs=[
            pl.BlockSpec((tile, D_MODEL), lambda i: (i, 0)),
            pl.BlockSpec((N_ADA, 1, D_MODEL), lambda i: (0, 0, 0)),
            full((1, D_MODEL)),
            full((D_MODEL, n_main)),
            full((D_MODEL, LORA_PAD)),
            full((1, 3 * RWKV_WIDTH)),
            full((1, LORA_PAD)),
            full((CONV_K, CONV_WIDTH)),
            full((1, RWKV_WIDTH)),
            full((1, RWKV_WIDTH)),
            full((LORA_PAD, 3 * RWKV_WIDTH)),
            full((1, RWKV_WIDTH)),
            full((1, RWKV_WIDTH)),
            full((1, RWKV_WIDTH)),
            full((1, RWKV_WIDTH)),
            full((1, RWKV_WIDTH)),
            full((MXU_DIM, MXU_DIM)),
            full((D_MODEL, D_MODEL)),
            full((1, D_MODEL)),
        ] + cast_in,
        out_specs=[pl.BlockSpec((tile, D_MODEL), lambda i: (i, 0))] + cast_out,
        out_shape=[jax.ShapeDtypeStruct((t, D_MODEL), F32)] + cast_shapes,
        scratch_shapes=[
            pltpu.VMEM((1, 3 * RWKV_WIDTH), F32),
            pltpu.VMEM((1, LORA_PAD), F32),
            pltpu.VMEM((2, CONV_WIDTH), F32),
            pltpu.VMEM((N_PAIRS, LANES, LANES), F32),
        ],
        compiler_params=pltpu.CompilerParams(dimension_semantics=("arbitrary",),
                                             vmem_limit_bytes=VMEM_LIMIT),
        name="token_mixing",
    )(h, ada, row(g_pre), w_main, w_lo, mu_main, mu_lo, conv_w, row(w0), row(a0),
      lora_up.astype(BF16), row(k_k), row(k_a), row(r_k), row(ln_x_w), row(ln_x_b), ones_bd,
      w_out, row(g_post), *[arr for arr, _ in casts])
    return out[0], out[1:]


def kernel(x, c, w_ada, b_ada, ffn1_g_pre, ffn1_w1, ffn1_w3, ffn1_w2, ffn1_g_post, mix_g_pre, w_in, conv_w, mu_shift, w0, w_up, a0, a_up, g_up, k_k, k_a, r_k, ln_x_w, ln_x_b, w_out, mix_g_post, ffn2_g_pre, ffn2_w1, ffn2_w3, ffn2_w2, ffn2_g_post):
    bsz, t, _ = x.shape
    outs = []
    for b in range(bsz):
        h = x[b]
        for l in range(w_ada.shape[0]):
            ada = _ada(c[b], w_ada[l], b_ada[l])
            h, (w_main, w_out_b) = _ffn(
                h, ada, 0, ffn1_g_pre[l], ffn1_w1[l], ffn1_w3[l], ffn1_w2[l], ffn1_g_post[l],
                casts=[(w_in[l], MIX_MAIN_COLS), (w_out[l], D_MODEL)])
            h, (w1_b, w3_b, w2_b) = _mixer(
                h, ada, mix_g_pre[l], w_main, w_in[l], conv_w[l], mu_shift[l], w0[l], w_up[l],
                a0[l], a_up[l], g_up[l], k_k[l], k_a[l], r_k[l], ln_x_w[l], ln_x_b[l],
                w_out_b, mix_g_post[l],
                casts=[(ffn2_w1[l], D_FF), (ffn2_w3[l], D_FF), (ffn2_w2[l], D_MODEL)])
            h, _ = _ffn(h, ada, 6, ffn2_g_pre[l], w1_b, w3_b, w2_b, ffn2_g_post[l])
        outs.append(h)
    return jnp.stack(outs, axis=0)
```

```python
import functools

import jax
import jax.numpy as jnp
from jax import lax
from jax.experimental import pallas as pl
from jax.experimental.pallas import tpu as pltpu

F32 = jnp.float32
BF16 = jnp.bfloat16

D_MODEL = 1024
D_FF = 2816
CONV_WIDTH = 512
CONV_K = 3
RWKV_WIDTH = 512
RWKV_HEAD = 64
RWKV_HEADS = RWKV_WIDTH // RWKV_HEAD
DECAY_LORA = 32
AAA_LORA = 32
GATE_LORA = 96
LORA_COLS = DECAY_LORA + AAA_LORA + GATE_LORA
N_ADA = 9
MACARON_W = 0.5
NORM_EPS = 1e-6
GN_EPS = 64e-5

LANES = 128
SUBLANES = 8
MXU_DIM = 256
LORA_PAD = MXU_DIM
BF16_ROWS = 16
MIX_MAIN_COLS = 3 * CONV_WIDTH + 3 * RWKV_WIDTH
MIX_INPUTS = 19
HEADS_PER_PAIR = LANES // RWKV_HEAD
N_PAIRS = RWKV_WIDTH // LANES
CHUNK = 64
SUB = 8
MIX_TILE = 512
MIX_SUBTILE = 256
MIX_PHASES = 5
MIX_LAG = 8
MIX_STEPS = 8
FFN_TILE = 1024
FFN_SUBTILE = 256
FFN_LAG = 4
FFN_COLS = MXU_DIM
ADA_ROWS = 128
VMEM_LIMIT = 56 * 1024 * 1024


def _rms(x):
    return x * lax.rsqrt(jnp.mean(x * x, axis=-1, keepdims=True) + NORM_EPS)


def _dot(a, b):
    return jnp.dot(a.astype(BF16), b.astype(BF16), preferred_element_type=F32)


def _ada_kernel(c_ref, w_ref, b_ref, o_ref):
    @pl.when(pl.program_id(0) == 0)
    def _():
        o_ref[...] = b_ref[...]

    c = c_ref[...]
    cond = c * jax.nn.sigmoid(c)
    part = jnp.sum(w_ref[...] * cond, axis=0, keepdims=True)
    for k in range(N_ADA):
        o_ref[k] += part[:, k * D_MODEL:(k + 1) * D_MODEL]


def _ada(c, w_ada, b_ada):
    full = pl.BlockSpec((N_ADA, 1, D_MODEL), lambda j: (0, 0, 0))
    return pl.pallas_call(
        _ada_kernel,
        grid=(D_MODEL // ADA_ROWS,),
        in_specs=[
            pl.BlockSpec((ADA_ROWS, 1), lambda j: (j, 0)),
            pl.BlockSpec((ADA_ROWS, N_ADA * D_MODEL), lambda j: (j, 0)),
            full,
        ],
        out_specs=full,
        out_shape=jax.ShapeDtypeStruct((N_ADA, 1, D_MODEL), F32),
        compiler_params=pltpu.CompilerParams(dimension_semantics=("arbitrary",)),
        name="ada_proj",
    )(c.reshape(D_MODEL, 1), w_ada, b_ada.reshape(N_ADA, 1, D_MODEL))


def _ffn_rows_steps(h_ref, o_ref, act_ref, row0, rows, c):
    x = h_ref[row0:row0 + rows, :]
    u = (_rms(x) * c["g_pre"] * (1.0 + c["scale"]) + c["shift"]).astype(BF16)
    yield
    for j in range(D_FF // FFN_COLS):
        cols = slice(j * FFN_COLS, (j + 1) * FFN_COLS)
        a = jnp.dot(u, c["w1"][:, cols], preferred_element_type=F32)
        b = jnp.dot(u, c["w3"][:, cols], preferred_element_type=F32)
        act_ref[row0:row0 + rows, cols] = (a * jax.nn.sigmoid(a) * b).astype(BF16)
        yield
    act = act_ref[row0:row0 + rows, :]
    y_cols = []
    for j in range(D_MODEL // FFN_COLS):
        cols = slice(j * FFN_COLS, (j + 1) * FFN_COLS)
        y_cols.append(jnp.dot(act, c["w2"][:, cols], preferred_element_type=F32))
        yield
    y = jnp.concatenate(y_cols, axis=1)
    o_ref[row0:row0 + rows, :] = x + MACARON_W * c["gate"] * (_rms(y) * c["g_post"])
    yield


def _cast_plan(casts, n_steps):
    in_specs, out_specs, out_shapes = [], [], []
    for arr, cols in casts:
        rows = arr.shape[0]
        n_blk = n_steps
        while rows % n_blk or (rows // n_blk) % BF16_ROWS:
            n_blk //= 2
        idx = lambda i, every=n_steps // n_blk: (i // every, 0)
        in_specs.append(pl.BlockSpec((rows // n_blk, arr.shape[1]), idx))
        out_specs.append(pl.BlockSpec((rows // n_blk, cols), idx))
        out_shapes.append(jax.ShapeDtypeStruct((rows, cols), BF16))
    return in_specs, out_specs, out_shapes


def _cast_blocks(src_refs, dst_refs):
    for src, dst in zip(src_refs, dst_refs):
        dst[...] = src[:, :dst.shape[1]].astype(BF16)


def _ffn_kernel(ada_row, n_cast, *refs):
    h_ref, ada_ref, gpre_ref, w1_ref, w3_ref, w2_ref, gpost_ref = refs[:7]
    o_ref, act_ref = refs[7 + n_cast], refs[-1]
    _cast_blocks(refs[7:7 + n_cast], refs[8 + n_cast:8 + 2 * n_cast])
    tile = h_ref.shape[0]
    sub = min(FFN_SUBTILE, tile)
    consts = dict(shift=ada_ref[ada_row], scale=ada_ref[ada_row + 1], gate=ada_ref[ada_row + 2],
                  g_pre=gpre_ref[...], g_post=gpost_ref[...], w1=w1_ref, w3=w3_ref, w2=w2_ref)
    blocks = [_ffn_rows_steps(h_ref, o_ref, act_ref, r0, sub, consts) for r0 in range(0, tile, sub)]
    n_steps = 2 + D_FF // FFN_COLS + D_MODEL // FFN_COLS
    for step in range((len(blocks) - 1) * FFN_LAG + n_steps):
        for k, blk in enumerate(blocks):
            if 0 <= step - k * FFN_LAG < n_steps:
                next(blk)


def _ffn(h, ada, ada_row, g_pre, w1, w3, w2, g_post, casts=()):
    t = h.shape[0]
    tile = min(FFN_TILE, t)
    const = lambda i: (0, 0)
    cast_in, cast_out, cast_shapes = _cast_plan(casts, t // tile)
    out = pl.pallas_call(
        functools.partial(_ffn_kernel, ada_row, len(casts)),
        grid=(t // tile,),
        in_specs=[
            pl.BlockSpec((tile, D_MODEL), lambda i: (i, 0)),
            pl.BlockSpec((N_ADA, 1, D_MODEL), lambda i: (0, 0, 0)),
            pl.BlockSpec((1, D_MODEL), const),
            pl.BlockSpec((D_MODEL, D_FF), const, pipeline_mode=pl.Buffered(1)),
            pl.BlockSpec((D_MODEL, D_FF), const, pipeline_mode=pl.Buffered(1)),
            pl.BlockSpec((D_FF, D_MODEL), const, pipeline_mode=pl.Buffered(1)),
            pl.BlockSpec((1, D_MODEL), const),
        ] + cast_in,
        out_specs=[pl.BlockSpec((tile, D_MODEL), lambda i: (i, 0))] + cast_out,
        out_shape=[jax.ShapeDtypeStruct((t, D_MODEL), F32)] + cast_shapes,
        scratch_shapes=[pltpu.VMEM((tile, D_FF), BF16)],
        compiler_params=pltpu.CompilerParams(dimension_semantics=("arbitrary",),
                                             vmem_limit_bytes=VMEM_LIMIT),
        name="swiglu_half_step",
    )(h, ada, g_pre.reshape(1, D_MODEL), w1.astype(BF16), w3.astype(BF16), w2.astype(BF16),
      g_post.reshape(1, D_MODEL), *[arr for arr, _ in casts])
    return out[0], out[1:]


def _shift_rows(x, fill_row):
    row = lax.broadcasted_iota(jnp.int32, x.shape, 0)
    return jnp.where(row == 0, fill_row, pltpu.roll(x, 1, 0))


def _softplus(z):
    return jnp.maximum(z, 0.0) + jnp.log(1.0 + jnp.exp(-jnp.abs(z)))


def _bmm(a, b):
    return jnp.einsum("bmk,bkn->bmn", a.astype(BF16), b.astype(BF16), preferred_element_type=F32)


def _bmm_nt(a, b):
    return jnp.einsum("bmk,bnk->bmn", a.astype(BF16), b.astype(BF16), preferred_element_type=F32)


def _bmm_tn(a, b):
    return jnp.einsum("bkm,bkn->bmn", a.astype(BF16), b.astype(BF16), preferred_element_type=F32)


def _block_diag(x, left):
    x = x.astype(BF16)
    zero = jnp.zeros_like(x)
    return jnp.concatenate([jnp.where(left, x, zero), jnp.where(left, zero, x)], axis=1)


def _chunk_terms_steps(ph, rh, qh, kh, pt, rt, qb, kb, v, masks, res):
    left, strict, incl, diag_blk, eye, bd_mask = masks
    bd = lambda x: _block_diag(x, left)
    mm = lambda a, b: _bmm(a, bd(b))
    ph_hi = ph.astype(BF16)
    ph_lo = ph - ph_hi.astype(F32)
    gram3 = _bmm_nt(jnp.concatenate([ph_hi, rh, ph_lo], axis=1),
                    jnp.concatenate([bd(qh), bd(kh)], axis=1))
    gram = jnp.concatenate([gram3[:, :CHUNK] + gram3[:, 2 * CHUNK:], gram3[:, CHUNK:2 * CHUNK]],
                           axis=1)
    a_pq = jnp.where(strict, gram[:, :CHUNK, :LANES], 0.0)
    a_rq = jnp.where(incl, gram[:, CHUNK:, :LANES], 0.0)
    a_pk = jnp.where(strict, gram[:, :CHUNK, LANES:], 0.0)
    a_rk = jnp.where(incl, gram[:, CHUNK:, LANES:], 0.0)
    yield
    a_d = jnp.where(diag_blk, a_pq, 0.0)
    a_o = a_pq - a_d
    x2 = mm(a_d, a_d)
    t_d = eye + a_d
    yield
    both = mm(jnp.concatenate([x2, t_d], axis=1), x2)
    x4, t_d = both[:, :CHUNK], t_d + both[:, CHUNK:]
    yield
    t_d = t_d + mm(t_d, x4)
    n1 = mm(t_d, a_o)
    yield
    n2 = mm(n1, n1)
    s = eye + n1
    yield
    both = mm(jnp.concatenate([n2, s], axis=1), n2)
    n4, s = both[:, :CHUNK], s + both[:, CHUNK:]
    yield
    s = s + mm(s, n4)
    t_inv = mm(s, t_d)
    both = mm(jnp.concatenate([a_pk, a_rk], axis=1), v)
    av, ark_v = both[:, :CHUNK], both[:, CHUNK:]
    yield
    av_hi = av.astype(BF16)
    av_lo = av - av_hi.astype(F32)
    wu = _bmm(t_inv, jnp.concatenate([bd(pt), bd(av_hi), bd(av_lo)], axis=2))
    w, ub = wu[:, :, :LANES], wu[:, :, LANES:2 * LANES] + wu[:, :, 2 * LANES:]
    ru = _bmm(a_rq, jnp.concatenate([bd(w), bd(ub)], axis=2))
    res["rw"] = rt + ru[:, :, :LANES]
    res["y0"] = ru[:, :, LANES:] + ark_v
    res["g_off"] = jnp.where(bd_mask, _bmm_tn(qb, w), 0.0)
    res["c_bd"] = jnp.where(bd_mask, _bmm_tn(jnp.concatenate([qb, kb], axis=1),
                                             jnp.concatenate([ub, v], axis=1)), 0.0)
    yield


def _seg_sum(z, ones_bd):
    half = ones_bd.shape[0]
    return jnp.concatenate([_dot(z[:, :half], ones_bd), _dot(z[:, half:], ones_bd)], axis=1)


def _mix_rows_steps(k, h_ref, o_ref, row0, rows, shared, c):
    n_chunks = rows // CHUNK
    cw, rwd = CONV_WIDTH, RWKV_WIDTH
    x = h_ref[row0:row0 + rows, :]
    u = (_rms(x) * c["g_pre"] * (1.0 + c["scale"]) + c["shift"]).astype(BF16)
    yield
    p_cols = []
    for j in range(6):
        p_cols.append(jnp.dot(u, c["w_in"][:, j * cw:(j + 1) * cw], preferred_element_type=F32))
        yield
    plo = jnp.dot(u, c["w_lo"][...], preferred_element_type=F32)
    c_pre, c_post, c_val = p_cols[0], p_cols[1], p_cols[2]
    rw_raw = jnp.concatenate(p_cols[3:], axis=1)
    cv = c_pre * c_val
    shared[k] = dict(prev_rw=rw_raw[rows - 1:, :], prev_lo=plo[rows - 1:, :],
                     prev_cv1=cv[rows - 1:, :], prev_cv2=cv[rows - 2:rows - 1, :])
    yield

    prev = shared[k - 1]
    conv_w = c["conv_w"]
    cv1 = _shift_rows(cv, prev["prev_cv1"])
    cv2 = _shift_rows(cv1, prev["prev_cv2"])
    y_conv = c_post * (conv_w[0:1, :] * cv2 + conv_w[1:2, :] * cv1 + conv_w[2:3, :] * cv)
    yield
    rw_mix = rw_raw + (_shift_rows(rw_raw, prev["prev_rw"]) - rw_raw) * c["mu"]
    lo_mix = plo + (_shift_rows(plo, prev["prev_lo"]) - plo) * c["mu_lo"]
    xr, xk, xv = rw_mix[:, :rwd], rw_mix[:, rwd:2 * rwd], rw_mix[:, 2 * rwd:]
    yield
    lane_lo = lax.broadcasted_iota(jnp.int32, lo_mix.shape, 1)
    lo_act = jnp.where(lane_lo < DECAY_LORA, jnp.tanh(lo_mix),
                       jnp.where(lane_lo < DECAY_LORA + AAA_LORA, lo_mix, jax.nn.sigmoid(lo_mix)))
    lup = _dot(lo_act, c["lora_up"][...])
    yield
    w_raw = -_softplus(-(c["w0"] + lup[:, :rwd])) - 0.5
    logw = -jnp.exp(w_raw)
    a = jax.nn.sigmoid(c["a0"] + lup[:, rwd:2 * rwd])
    g = lup[:, 2 * rwd:]
    yield
    ones_bd = c["ones"][...]
    kk = xk * c["k_k"]
    kk = kk / jnp.maximum(jnp.sqrt(_seg_sum(kk * kk, ones_bd)), 1e-12)
    kmod = xk * (1.0 + (a - 1.0) * c["k_a"])
    bonus = _seg_sum(xr * kmod * c["r_k"], ones_bd) * xv
    pv = -kk
    qv = kk * a
    yield
    logw_hi = logw.astype(BF16)
    logw_lo = (logw - logw_hi.astype(F32)).astype(BF16)
    gc = (jnp.dot(c["dmat"], logw_hi, preferred_element_type=F32)
          + jnp.dot(c["dmat"], logw_lo, preferred_element_type=F32))
    e_fwd = jnp.exp(gc)
    e_bwd = jnp.exp(-gc)
    e_prev = jnp.exp(gc - logw)
    yield
    per_chunk = lambda z: z.reshape(n_chunks, CHUNK, rwd)
    gc3, lw3 = per_chunk(gc), per_chunk(logw)
    g_mid = lw3[:, 0:1, :] - gc3[:, 0:1, :]
    g_end = gc3[:, CHUNK - 1:CHUNK, :]
    e_mid = jnp.exp(g_mid)
    e_end = jnp.exp(g_end)
    decay_end = jnp.exp(g_mid + g_end)
    scale_chunk = lambda z, f: (per_chunk(z) * f).reshape(rows, rwd)
    ph = pv * e_prev
    rh = xr * e_fwd
    qh = qv * e_bwd
    kh = kmod * e_bwd
    yield
    pt = scale_chunk(ph, e_mid)
    rt = scale_chunk(rh, e_mid)
    qb = scale_chunk(qh, e_end)
    kb = scale_chunk(kh, e_end)

    def batch(z):
        return jnp.stack([z[ci * CHUNK:(ci + 1) * CHUNK, j * LANES:(j + 1) * LANES]
                          for ci in range(n_chunks) for j in range(N_PAIRS)], axis=0)

    terms = {}
    chunk_steps = _chunk_terms_steps(batch(ph), batch(rh), batch(qh), batch(kh), batch(pt),
                                     batch(rt), batch(qb), batch(kb), batch(xv), c["masks"], terms)
    yield

    for _ in chunk_steps:
        yield

    m = shared[k - 1]["state"]
    o_rows = []
    for ci in range(n_chunks):
        b = slice(ci * N_PAIRS, (ci + 1) * N_PAIRS)
        o_c = _bmm(terms["rw"][b], m) + terms["y0"][b]
        o_rows.append(jnp.concatenate([o_c[j] for j in range(N_PAIRS)], axis=1))
        dec = jnp.stack([jnp.where(c["eye2"], decay_end[ci, :, j * LANES:(j + 1) * LANES], 0.0)
                         for j in range(N_PAIRS)], axis=0)
        m = jnp.sum(dec, axis=2, keepdims=True) * m + _bmm(terms["g_off"][b], m) + terms["c_bd"][b]
        if ci == n_chunks - 1:
            shared[k]["state"] = m
        yield
    o = jnp.concatenate(o_rows, axis=0)
    inv_n = 1.0 / RWKV_HEAD
    o_c = o - _seg_sum(o, ones_bd) * inv_n
    yield
    o_var = _seg_sum(o_c * o_c, ones_bd) * inv_n
    o_n = o_c * lax.rsqrt(o_var + GN_EPS) * c["ln_w"] + c["ln_b"]
    y_rwkv = (o_n + bonus) * g
    y_in = jnp.concatenate([y_conv, y_rwkv], axis=1).astype(BF16)
    yield
    yield
    yield

    y_cols = []
    for j in range(D_MODEL // MXU_DIM):
        y_cols.append(jnp.dot(y_in, c["w_out"][:, j * MXU_DIM:(j + 1) * MXU_DIM],
                              preferred_element_type=F32))
        yield
    y = jnp.concatenate(y_cols, axis=1)
    o_ref[row0:row0 + rows, :] = x + c["gate"] * (_rms(y) * c["g_post"])
    yield
    yield
    yield
    yield


def _mixer_kernel(n_cast, *refs):
    (h_ref, ada_ref, gpre_ref, win_ref, wlo_ref, mu_ref, mulo_ref, convw_ref, w0_ref, a0_ref,
     loraup_ref, kk_ref, ka_ref, rk_ref, lnw_ref, lnb_ref, ones_ref, wout_ref,
     gpost_ref) = refs[:MIX_INPUTS]
    o_ref = refs[MIX_INPUTS + n_cast]
    prev_rw_ref, prev_lo_ref, prev_cv_ref, state_ref = refs[-4:]
    _cast_blocks(refs[MIX_INPUTS:MIX_INPUTS + n_cast],
                 refs[MIX_INPUTS + n_cast + 1:MIX_INPUTS + 2 * n_cast + 1])
    tile = h_ref.shape[0]
    sub = min(MIX_SUBTILE, tile)
    assert sub // CHUNK == MIX_STEPS // 2

    @pl.when(pl.program_id(0) == 0)
    def _():
        prev_rw_ref[...] = jnp.zeros_like(prev_rw_ref)
        prev_lo_ref[...] = jnp.zeros_like(prev_lo_ref)
        prev_cv_ref[...] = jnp.zeros_like(prev_cv_ref)
        state_ref[...] = jnp.zeros_like(state_ref)

    ti = lax.broadcasted_iota(jnp.int32, (sub, sub), 0)
    si = lax.broadcasted_iota(jnp.int32, (sub, sub), 1)
    tl, sl = ti % CHUNK, si % CHUNK
    mid = CHUNK // 2 - 1
    dmat = jnp.where((ti // CHUNK) == (si // CHUNK),
                     (sl <= tl).astype(F32) - (sl <= mid).astype(F32), 0.0).astype(BF16)
    r_i = lax.broadcasted_iota(jnp.int32, (CHUNK, LANES), 0)
    l_i = lax.broadcasted_iota(jnp.int32, (CHUNK, LANES), 1)
    col = l_i % RWKV_HEAD
    r2 = lax.broadcasted_iota(jnp.int32, (LANES, LANES), 0)
    c2 = lax.broadcasted_iota(jnp.int32, (LANES, LANES), 1)
    masks = (l_i < RWKV_HEAD, r_i > col, r_i >= col, (r_i // SUB) == (col // SUB),
             (r_i == col).astype(F32), (r2 // RWKV_HEAD) == (c2 // RWKV_HEAD))
    consts = dict(
        shift=ada_ref[3], scale=ada_ref[4], gate=ada_ref[5], g_pre=gpre_ref[...],
        w_in=win_ref, w_lo=wlo_ref, mu=mu_ref[...], mu_lo=mulo_ref[...], conv_w=convw_ref[...],
        w0=w0_ref[...], a0=a0_ref[...], lora_up=loraup_ref, k_k=kk_ref[...], k_a=ka_ref[...],
        r_k=rk_ref[...], ln_w=lnw_ref[...], ln_b=lnb_ref[...], ones=ones_ref, w_out=wout_ref,
        g_post=gpost_ref[...], dmat=dmat, masks=masks, eye2=r2 == c2)

    n_sub = tile // sub
    shared = {-1: dict(prev_rw=prev_rw_ref[...], prev_lo=prev_lo_ref[...],
                       prev_cv1=prev_cv_ref[0:1, :], prev_cv2=prev_cv_ref[1:2, :],
                       state=state_ref[...])}
    blocks = [_mix_rows_steps(k, h_ref, o_ref, k * sub, sub, shared, consts) for k in range(n_sub)]
    for step in range((n_sub - 1) * MIX_LAG + MIX_PHASES * MIX_STEPS):
        for k, blk in enumerate(blocks):
            if 0 <= step - k * MIX_LAG < MIX_PHASES * MIX_STEPS:
                next(blk)
    final = shared[n_sub - 1]
    prev_rw_ref[...] = final["prev_rw"]
    prev_lo_ref[...] = final["prev_lo"]
    prev_cv_ref[0:1, :] = final["prev_cv1"]
    prev_cv_ref[1:2, :] = final["prev_cv2"]
    state_ref[...] = final["state"]


def _mixer(h, ada, g_pre, w_main, w_in, conv_w, mu_shift, w0, w_up, a0, a_up, g_up, k_k, k_a, r_k,
           ln_x_w, ln_x_b, w_out, g_post, casts=()):
    t = h.shape[0]
    tile = min(MIX_TILE, t)
    n_main = MIX_MAIN_COLS
    w_lo = jnp.pad(w_in[:, n_main:], ((0, 0), (0, LORA_PAD - LORA_COLS))).astype(BF16)
    mu_main = mu_shift[:3 * RWKV_WIDTH].reshape(1, -1)
    mu_lo = jnp.pad(mu_shift[3 * RWKV_WIDTH:], (0, LORA_PAD - LORA_COLS)).reshape(1, -1)
    lora_up = jnp.zeros((LORA_PAD, 3 * RWKV_WIDTH), F32)
    lora_up = lora_up.at[:DECAY_LORA, :RWKV_WIDTH].set(w_up)
    lora_up = lora_up.at[DECAY_LORA:DECAY_LORA + AAA_LORA, RWKV_WIDTH:2 * RWKV_WIDTH].set(a_up)
    lora_up = lora_up.at[DECAY_LORA + AAA_LORA:LORA_COLS, 2 * RWKV_WIDTH:].set(g_up)
    head_id = jnp.arange(MXU_DIM) // RWKV_HEAD
    ones_bd = (head_id[:, None] == head_id[None, :]).astype(BF16)
    row = lambda z: z.reshape(1, -1)
    const = lambda i: (0, 0)
    full = lambda shape: pl.BlockSpec(shape, const)
    cast_in, cast_out, cast_shapes = _cast_plan(casts, t // tile)
    out = pl.pallas_call(
        functools.partial(_mixer_kernel, len(casts)),
        grid=(t // tile,),
        in_specs=[
            pl.BlockSpec((tile, D_MODEL), lambda i: (i, 0)),
            pl.BlockSpec((N_ADA, 1, D_MODEL), lambda i: (0, 0, 0)),
            full((1, D_MODEL)),
            full((D_MODEL, n_main)),
            full((D_MODEL, LORA_PAD)),
            full((1, 3 * RWKV_WIDTH)),
            full((1, LORA_PAD)),
            full((CONV_K, CONV_WIDTH)),
            full((1, RWKV_WIDTH)),
            full((1, RWKV_WIDTH)),
            full((LORA_PAD, 3 * RWKV_WIDTH)),
            full((1, RWKV_WIDTH)),
            full((1, RWKV_WIDTH)),
            full((1, RWKV_WIDTH)),
            full((1, RWKV_WIDTH)),
            full((1, RWKV_WIDTH)),
            full((MXU_DIM, MXU_DIM)),
            full((D_MODEL, D_MODEL)),
            full((1, D_MODEL)),
        ] + cast_in,
        out_specs=[pl.BlockSpec((tile, D_MODEL), lambda i: (i, 0))] + cast_out,
        out_shape=[jax.ShapeDtypeStruct((t, D_MODEL), F32)] + cast_shapes,
        scratch_shapes=[
            pltpu.VMEM((1, 3 * RWKV_WIDTH), F32),
            pltpu.VMEM((1, LORA_PAD), F32),
            pltpu.VMEM((2, CONV_WIDTH), F32),
            pltpu.VMEM((N_PAIRS, LANES, LANES), F32),
        ],
        compiler_params=pltpu.CompilerParams(dimension_semantics=("arbitrary",),
                                             vmem_limit_bytes=VMEM_LIMIT),
        name="token_mixing",
    )(h, ada, row(g_pre), w_main, w_lo, mu_main, mu_lo, conv_w, row(w0), row(a0),
      lora_up.astype(BF16), row(k_k), row(k_a), row(r_k), row(ln_x_w), row(ln_x_b), ones_bd,
      w_out, row(g_post), *[arr for arr, _ in casts])
    return out[0], out[1:]


def kernel(x, c, w_ada, b_ada, ffn1_g_pre, ffn1_w1, ffn1_w3, ffn1_w2, ffn1_g_post, mix_g_pre, w_in, conv_w, mu_shift, w0, w_up, a0, a_up, g_up, k_k, k_a, r_k, ln_x_w, ln_x_b, w_out, mix_g_post, ffn2_g_pre, ffn2_w1, ffn2_w3, ffn2_w2, ffn2_g_post):
    bsz, t, _ = x.shape
    outs = []
    for b in range(bsz):
        h = x[b]
        for l in range(w_ada.shape[0]):
            ada = _ada(c[b], w_ada[l], b_ada[l])
            h, (w_main, w_out_b) = _ffn(
                h, ada, 0, ffn1_g_pre[l], ffn1_w1[l], ffn1_w3[l], ffn1_w2[l], ffn1_g_post[l],
                casts=[(w_in[l], MIX_MAIN_COLS), (w_out[l], D_MODEL)])
            h, (w1_b, w3_b, w2_b) = _mixer(
                h, ada, mix_g_pre[l], w_main, w_in[l], conv_w[l], mu_shift[l], w0[l], w_up[l],
                a0[l], a_up[l], g_up[l], k_k[l], k_a[l], r_k[l], ln_x_w[l], ln_x_b[l],
                w_out_b, mix_g_post[l],
                casts=[(ffn2_w1[l], D_FF), (ffn2_w3[l], D_FF), (ffn2_w2[l], D_MODEL)])
            h, _ = _ffn(h, ada, 6, ffn2_g_pre[l], w1_b, w3_b, w2_b, ffn2_g_post[l])
        outs.append(h)
    return jnp.stack(outs, axis=0)
```

```python
import functools

import jax
import jax.numpy as jnp
from jax import lax
from jax.experimental import pallas as pl
from jax.experimental.pallas import tpu as pltpu

F32 = jnp.float32
BF16 = jnp.bfloat16

D_MODEL = 1024
D_FF = 2816
CONV_WIDTH = 512
CONV_K = 3
RWKV_WIDTH = 512
RWKV_HEAD = 64
RWKV_HEADS = RWKV_WIDTH // RWKV_HEAD
DECAY_LORA = 32
AAA_LORA = 32
GATE_LORA = 96
LORA_COLS = DECAY_LORA + AAA_LORA + GATE_LORA
N_ADA = 9
MACARON_W = 0.5
NORM_EPS = 1e-6
GN_EPS = 64e-5

LANES = 128
SUBLANES = 8
MXU_DIM = 256
LORA_PAD = MXU_DIM
BF16_ROWS = 16
MIX_MAIN_COLS = 3 * CONV_WIDTH + 3 * RWKV_WIDTH
MIX_INPUTS = 19
HEADS_PER_PAIR = LANES // RWKV_HEAD
N_PAIRS = RWKV_WIDTH // LANES
CHUNK = 64
SUB = 8
MIX_TILE = 1024
MIX_SUBTILE = 256
MIX_PHASES = 5
MIX_LAG = 8
MIX_STEPS = 8
FFN_TILE = 1024
FFN_SUBTILE = 256
FFN_LAG = 4
FFN_COLS = MXU_DIM
ADA_ROWS = 128
VMEM_LIMIT = 56 * 1024 * 1024


def _rms(x):
    return x * lax.rsqrt(jnp.mean(x * x, axis=-1, keepdims=True) + NORM_EPS)


def _dot(a, b):
    return jnp.dot(a.astype(BF16), b.astype(BF16), preferred_element_type=F32)


def _ada_kernel(c_ref, w_ref, b_ref, o_ref):
    @pl.when(pl.program_id(0) == 0)
    def _():
        o_ref[...] = b_ref[...]

    c = c_ref[...]
    cond = c * jax.nn.sigmoid(c)
    part = jnp.sum(w_ref[...] * cond, axis=0, keepdims=True)
    for k in range(N_ADA):
        o_ref[k] += part[:, k * D_MODEL:(k + 1) * D_MODEL]


def _ada(c, w_ada, b_ada):
    full = pl.BlockSpec((N_ADA, 1, D_MODEL), lambda j: (0, 0, 0))
    return pl.pallas_call(
        _ada_kernel,
        grid=(D_MODEL // ADA_ROWS,),
        in_specs=[
            pl.BlockSpec((ADA_ROWS, 1), lambda j: (j, 0)),
            pl.BlockSpec((ADA_ROWS, N_ADA * D_MODEL), lambda j: (j, 0)),
            full,
        ],
        out_specs=full,
        out_shape=jax.ShapeDtypeStruct((N_ADA, 1, D_MODEL), F32),
        compiler_params=pltpu.CompilerParams(dimension_semantics=("arbitrary",)),
        name="ada_proj",
    )(c.reshape(D_MODEL, 1), w_ada, b_ada.reshape(N_ADA, 1, D_MODEL))


def _ffn_rows_steps(h_ref, o_ref, act_ref, row0, rows, c):
    x = h_ref[row0:row0 + rows, :]
    u = (_rms(x) * c["g_pre"] * (1.0 + c["scale"]) + c["shift"]).astype(BF16)
    yield
    for j in range(D_FF // FFN_COLS):
        cols = slice(j * FFN_COLS, (j + 1) * FFN_COLS)
        a = jnp.dot(u, c["w1"][:, cols], preferred_element_type=F32)
        b = jnp.dot(u, c["w3"][:, cols], preferred_element_type=F32)
        act_ref[row0:row0 + rows, cols] = (a * jax.nn.sigmoid(a) * b).astype(BF16)
        yield
    act = act_ref[row0:row0 + rows, :]
    y_cols = []
    for j in range(D_MODEL // FFN_COLS):
        cols = slice(j * FFN_COLS, (j + 1) * FFN_COLS)
        y_cols.append(jnp.dot(act, c["w2"][:, cols], preferred_element_type=F32))
        yield
    y = jnp.concatenate(y_cols, axis=1)
    o_ref[row0:row0 + rows, :] = x + MACARON_W * c["gate"] * (_rms(y) * c["g_post"])
    yield


def _cast_plan(casts, n_steps):
    in_specs, out_specs, out_shapes = [], [], []
    for arr, cols in casts:
        rows = arr.shape[0]
        n_blk = n_steps
        while rows % n_blk or (rows // n_blk) % BF16_ROWS:
            n_blk //= 2
        idx = lambda i, every=n_steps // n_blk: (i // every, 0)
        in_specs.append(pl.BlockSpec((rows // n_blk, arr.shape[1]), idx))
        out_specs.append(pl.BlockSpec((rows // n_blk, cols), idx))
        out_shapes.append(jax.ShapeDtypeStruct((rows, cols), BF16))
    return in_specs, out_specs, out_shapes


def _cast_blocks(src_refs, dst_refs):
    for src, dst in zip(src_refs, dst_refs):
        dst[...] = src[:, :dst.shape[1]].astype(BF16)


def _ffn_kernel(ada_row, n_cast, *refs):
    h_ref, ada_ref, gpre_ref, w1_ref, w3_ref, w2_ref, gpost_ref = refs[:7]
    o_ref, act_ref = refs[7 + n_cast], refs[-1]
    _cast_blocks(refs[7:7 + n_cast], refs[8 + n_cast:8 + 2 * n_cast])
    tile = h_ref.shape[0]
    sub = min(FFN_SUBTILE, tile)
    consts = dict(shift=ada_ref[ada_row], scale=ada_ref[ada_row + 1], gate=ada_ref[ada_row + 2],
                  g_pre=gpre_ref[...], g_post=gpost_ref[...], w1=w1_ref, w3=w3_ref, w2=w2_ref)
    blocks = [_ffn_rows_steps(h_ref, o_ref, act_ref, r0, sub, consts) for r0 in range(0, tile, sub)]
    n_steps = 2 + D_FF // FFN_COLS + D_MODEL // FFN_COLS
    for step in range((len(blocks) - 1) * FFN_LAG + n_steps):
        for k, blk in enumerate(blocks):
            if 0 <= step - k * FFN_LAG < n_steps:
                next(blk)


def _ffn(h, ada, ada_row, g_pre, w1, w3, w2, g_post, casts=()):
    t = h.shape[0]
    tile = min(FFN_TILE, t)
    const = lambda i: (0, 0)
    cast_in, cast_out, cast_shapes = _cast_plan(casts, t // tile)
    out = pl.pallas_call(
        functools.partial(_ffn_kernel, ada_row, len(casts)),
        grid=(t // tile,),
        in_specs=[
            pl.BlockSpec((tile, D_MODEL), lambda i: (i, 0)),
            pl.BlockSpec((N_ADA, 1, D_MODEL), lambda i: (0, 0, 0)),
            pl.BlockSpec((1, D_MODEL), const),
            pl.BlockSpec((D_MODEL, D_FF), const, pipeline_mode=pl.Buffered(1)),
            pl.BlockSpec((D_MODEL, D_FF), const, pipeline_mode=pl.Buffered(1)),
            pl.BlockSpec((D_FF, D_MODEL), const, pipeline_mode=pl.Buffered(1)),
            pl.BlockSpec((1, D_MODEL), const),
        ] + cast_in,
        out_specs=[pl.BlockSpec((tile, D_MODEL), lambda i: (i, 0))] + cast_out,
        out_shape=[jax.ShapeDtypeStruct((t, D_MODEL), F32)] + cast_shapes,
        scratch_shapes=[pltpu.VMEM((tile, D_FF), BF16)],
        compiler_params=pltpu.CompilerParams(dimension_semantics=("arbitrary",),
                                             vmem_limit_bytes=VMEM_LIMIT),
        name="swiglu_half_step",
    )(h, ada, g_pre.reshape(1, D_MODEL), w1.astype(BF16), w3.astype(BF16), w2.astype(BF16),
      g_post.reshape(1, D_MODEL), *[arr for arr, _ in casts])
    return out[0], out[1:]


def _shift_rows(x, fill_row):
    row = lax.broadcasted_iota(jnp.int32, x.shape, 0)
    return jnp.where(row == 0, fill_row, pltpu.roll(x, 1, 0))


def _softplus(z):
    return jnp.maximum(z, 0.0) + jnp.log(1.0 + jnp.exp(-jnp.abs(z)))


def _bmm(a, b):
    return jnp.einsum("bmk,bkn->bmn", a.astype(BF16), b.astype(BF16), preferred_element_type=F32)


def _bmm_nt(a, b):
    return jnp.einsum("bmk,bnk->bmn", a.astype(BF16), b.astype(BF16), preferred_element_type=F32)


def _bmm_tn(a, b):
    return jnp.einsum("bkm,bkn->bmn", a.astype(BF16), b.astype(BF16), preferred_element_type=F32)


def _block_diag(x, left):
    x = x.astype(BF16)
    zero = jnp.zeros_like(x)
    return jnp.concatenate([jnp.where(left, x, zero), jnp.where(left, zero, x)], axis=1)


def _chunk_terms_steps(ph, rh, qh, kh, pt, rt, qb, kb, v, masks, res):
    left, strict, incl, diag_blk, eye, bd_mask = masks
    bd = lambda x: _block_diag(x, left)
    mm = lambda a, b: _bmm(a, bd(b))
    ph_hi = ph.astype(BF16)
    ph_lo = ph - ph_hi.astype(F32)
    gram3 = _bmm_nt(jnp.concatenate([ph_hi, rh, ph_lo], axis=1),
                    jnp.concatenate([bd(qh), bd(kh)], axis=1))
    gram = jnp.concatenate([gram3[:, :CHUNK] + gram3[:, 2 * CHUNK:], gram3[:, CHUNK:2 * CHUNK]],
                           axis=1)
    a_pq = jnp.where(strict, gram[:, :CHUNK, :LANES], 0.0)
    a_rq = jnp.where(incl, gram[:, CHUNK:, :LANES], 0.0)
    a_pk = jnp.where(strict, gram[:, :CHUNK, LANES:], 0.0)
    a_rk = jnp.where(incl, gram[:, CHUNK:, LANES:], 0.0)
    yield
    a_d = jnp.where(diag_blk, a_pq, 0.0)
    a_o = a_pq - a_d
    x2 = mm(a_d, a_d)
    t_d = eye + a_d
    yield
    both = mm(jnp.concatenate([x2, t_d], axis=1), x2)
    x4, t_d = both[:, :CHUNK], t_d + both[:, CHUNK:]
    yield
    t_d = t_d + mm(t_d, x4)
    n1 = mm(t_d, a_o)
    yield
    n2 = mm(n1, n1)
    s = eye + n1
    yield
    both = mm(jnp.concatenate([n2, s], axis=1), n2)
    n4, s = both[:, :CHUNK], s + both[:, CHUNK:]
    yield
    s = s + mm(s, n4)
    t_inv = mm(s, t_d)
    both = mm(jnp.concatenate([a_pk, a_rk], axis=1), v)
    av, ark_v = both[:, :CHUNK], both[:, CHUNK:]
    yield
    av_hi = av.astype(BF16)
    av_lo = av - av_hi.astype(F32)
    wu = _bmm(t_inv, jnp.concatenate([bd(pt), bd(av_hi), bd(av_lo)], axis=2))
    w, ub = wu[:, :, :LANES], wu[:, :, LANES:2 * LANES] + wu[:, :, 2 * LANES:]
    ru = _bmm(a_rq, jnp.concatenate([bd(w), bd(ub)], axis=2))
    res["rw"] = rt + ru[:, :, :LANES]
    res["y0"] = ru[:, :, LANES:] + ark_v
    res["g_off"] = jnp.where(bd_mask, _bmm_tn(qb, w), 0.0)
    res["c_bd"] = jnp.where(bd_mask, _bmm_tn(jnp.concatenate([qb, kb], axis=1),
                                             jnp.concatenate([ub, v], axis=1)), 0.0)
    yield


def _seg_sum(z, ones_bd):
    half = ones_bd.shape[0]
    return jnp.concatenate([_dot(z[:, :half], ones_bd), _dot(z[:, half:], ones_bd)], axis=1)


def _mix_rows_steps(k, h_ref, o_ref, row0, rows, shared, c):
    n_chunks = rows // CHUNK
    cw, rwd = CONV_WIDTH, RWKV_WIDTH
    x = h_ref[row0:row0 + rows, :]
    u = (_rms(x) * c["g_pre"] * (1.0 + c["scale"]) + c["shift"]).astype(BF16)
    yield
    p_cols = []
    for j in range(6):
        p_cols.append(jnp.dot(u, c["w_in"][:, j * cw:(j + 1) * cw], preferred_element_type=F32))
        yield
    plo = jnp.dot(u, c["w_lo"][...], preferred_element_type=F32)
    c_pre, c_post, c_val = p_cols[0], p_cols[1], p_cols[2]
    rw_raw = jnp.concatenate(p_cols[3:], axis=1)
    cv = c_pre * c_val
    shared[k] = dict(prev_rw=rw_raw[rows - 1:, :], prev_lo=plo[rows - 1:, :],
                     prev_cv1=cv[rows - 1:, :], prev_cv2=cv[rows - 2:rows - 1, :])
    yield

    prev = shared[k - 1]
    conv_w = c["conv_w"]
    cv1 = _shift_rows(cv, prev["prev_cv1"])
    cv2 = _shift_rows(cv1, prev["prev_cv2"])
    y_conv = c_post * (conv_w[0:1, :] * cv2 + conv_w[1:2, :] * cv1 + conv_w[2:3, :] * cv)
    yield
    rw_mix = rw_raw + (_shift_rows(rw_raw, prev["prev_rw"]) - rw_raw) * c["mu"]
    lo_mix = plo + (_shift_rows(plo, prev["prev_lo"]) - plo) * c["mu_lo"]
    xr, xk, xv = rw_mix[:, :rwd], rw_mix[:, rwd:2 * rwd], rw_mix[:, 2 * rwd:]
    yield
    lane_lo = lax.broadcasted_iota(jnp.int32, lo_mix.shape, 1)
    lo_act = jnp.where(lane_lo < DECAY_LORA, jnp.tanh(lo_mix),
                       jnp.where(lane_lo < DECAY_LORA + AAA_LORA, lo_mix, jax.nn.sigmoid(lo_mix)))
    lup = _dot(lo_act, c["lora_up"][...])
    yield
    w_raw = -_softplus(-(c["w0"] + lup[:, :rwd])) - 0.5
    logw = -jnp.exp(w_raw)
    a = jax.nn.sigmoid(c["a0"] + lup[:, rwd:2 * rwd])
    g = lup[:, 2 * rwd:]
    yield
    ones_bd = c["ones"][...]
    kk = xk * c["k_k"]
    kk = kk / jnp.maximum(jnp.sqrt(_seg_sum(kk * kk, ones_bd)), 1e-12)
    kmod = xk * (1.0 + (a - 1.0) * c["k_a"])
    bonus = _seg_sum(xr * kmod * c["r_k"], ones_bd) * xv
    pv = -kk
    qv = kk * a
    yield
    logw_hi = logw.astype(BF16)
    logw_lo = (logw - logw_hi.astype(F32)).astype(BF16)
    gc = (jnp.dot(c["dmat"], logw_hi, preferred_element_type=F32)
          + jnp.dot(c["dmat"], logw_lo, preferred_element_type=F32))
    e_fwd = jnp.exp(gc)
    e_bwd = jnp.exp(-gc)
    e_prev = jnp.exp(gc - logw)
    yield
    per_chunk = lambda z: z.reshape(n_chunks, CHUNK, rwd)
    gc3, lw3 = per_chunk(gc), per_chunk(logw)
    g_mid = lw3[:, 0:1, :] - gc3[:, 0:1, :]
    g_end = gc3[:, CHUNK - 1:CHUNK, :]
    e_mid = jnp.exp(g_mid)
    e_end = jnp.exp(g_end)
    decay_end = jnp.exp(g_mid + g_end)
    scale_chunk = lambda z, f: (per_chunk(z) * f).reshape(rows, rwd)
    ph = pv * e_prev
    rh = xr * e_fwd
    qh = qv * e_bwd
    kh = kmod * e_bwd
    yield
    pt = scale_chunk(ph, e_mid)
    rt = scale_chunk(rh, e_mid)
    qb = scale_chunk(qh, e_end)
    kb = scale_chunk(kh, e_end)

    def batch(z):
        return jnp.stack([z[ci * CHUNK:(ci + 1) * CHUNK, j * LANES:(j + 1) * LANES]
                          for ci in range(n_chunks) for j in range(N_PAIRS)], axis=0)

    terms = {}
    chunk_steps = _chunk_terms_steps(batch(ph), batch(rh), batch(qh), batch(kh), batch(pt),
                                     batch(rt), batch(qb), batch(kb), batch(xv), c["masks"], terms)
    yield

    for _ in chunk_steps:
        yield

    m = shared[k - 1]["state"]
    o_rows = []
    for ci in range(n_chunks):
        b = slice(ci * N_PAIRS, (ci + 1) * N_PAIRS)
        o_c = _bmm(terms["rw"][b], m) + terms["y0"][b]
        o_rows.append(jnp.concatenate([o_c[j] for j in range(N_PAIRS)], axis=1))
        dec = jnp.stack([jnp.where(c["eye2"], decay_end[ci, :, j * LANES:(j + 1) * LANES], 0.0)
                         for j in range(N_PAIRS)], axis=0)
        m = jnp.sum(dec, axis=2, keepdims=True) * m + _bmm(terms["g_off"][b], m) + terms["c_bd"][b]
        if ci == n_chunks - 1:
            shared[k]["state"] = m
        yield
    o = jnp.concatenate(o_rows, axis=0)
    inv_n = 1.0 / RWKV_HEAD
    o_c = o - _seg_sum(o, ones_bd) * inv_n
    yield
    o_var = _seg_sum(o_c * o_c, ones_bd) * inv_n
    o_n = o_c * lax.rsqrt(o_var + GN_EPS) * c["ln_w"] + c["ln_b"]
    y_rwkv = (o_n + bonus) * g
    y_in = jnp.concatenate([y_conv, y_rwkv], axis=1).astype(BF16)
    yield
    yield
    yield

    y_cols = []
    for j in range(D_MODEL // MXU_DIM):
        y_cols.append(jnp.dot(y_in, c["w_out"][:, j * MXU_DIM:(j + 1) * MXU_DIM],
                              preferred_element_type=F32))
        yield
    y = jnp.concatenate(y_cols, axis=1)
    o_ref[row0:row0 + rows, :] = x + c["gate"] * (_rms(y) * c["g_post"])
    yield
    yield
    yield
    yield


def _mixer_kernel(n_cast, *refs):
    (h_ref, ada_ref, gpre_ref, win_ref, wlo_ref, mu_ref, mulo_ref, convw_ref, w0_ref, a0_ref,
     loraup_ref, kk_ref, ka_ref, rk_ref, lnw_ref, lnb_ref, ones_ref, wout_ref,
     gpost_ref) = refs[:MIX_INPUTS]
    o_ref = refs[MIX_INPUTS + n_cast]
    prev_rw_ref, prev_lo_ref, prev_cv_ref, state_ref = refs[-4:]
    _cast_blocks(refs[MIX_INPUTS:MIX_INPUTS + n_cast],
                 refs[MIX_INPUTS + n_cast + 1:MIX_INPUTS + 2 * n_cast + 1])
    tile = h_ref.shape[0]
    sub = min(MIX_SUBTILE, tile)
    assert sub // CHUNK == MIX_STEPS // 2

    @pl.when(pl.program_id(0) == 0)
    def _():
        prev_rw_ref[...] = jnp.zeros_like(prev_rw_ref)
        prev_lo_ref[...] = jnp.zeros_like(prev_lo_ref)
        prev_cv_ref[...] = jnp.zeros_like(prev_cv_ref)
        state_ref[...] = jnp.zeros_like(state_ref)

    ti = lax.broadcasted_iota(jnp.int32, (sub, sub), 0)
    si = lax.broadcasted_iota(jnp.int32, (sub, sub), 1)
    tl, sl = ti % CHUNK, si % CHUNK
    mid = CHUNK // 2 - 1
    dmat = jnp.where((ti // CHUNK) == (si // CHUNK),
                     (sl <= tl).astype(F32) - (sl <= mid).astype(F32), 0.0).astype(BF16)
    r_i = lax.broadcasted_iota(jnp.int32, (CHUNK, LANES), 0)
    l_i = lax.broadcasted_iota(jnp.int32, (CHUNK, LANES), 1)
    col = l_i % RWKV_HEAD
    r2 = lax.broadcasted_iota(jnp.int32, (LANES, LANES), 0)
    c2 = lax.broadcasted_iota(jnp.int32, (LANES, LANES), 1)
    masks = (l_i < RWKV_HEAD, r_i > col, r_i >= col, (r_i // SUB) == (col // SUB),
             (r_i == col).astype(F32), (r2 // RWKV_HEAD) == (c2 // RWKV_HEAD))
    consts = dict(
        shift=ada_ref[3], scale=ada_ref[4], gate=ada_ref[5], g_pre=gpre_ref[...],
        w_in=win_ref, w_lo=wlo_ref, mu=mu_ref[...], mu_lo=mulo_ref[...], conv_w=convw_ref[...],
        w0=w0_ref[...], a0=a0_ref[...], lora_up=loraup_ref, k_k=kk_ref[...], k_a=ka_ref[...],
        r_k=rk_ref[...], ln_w=lnw_ref[...], ln_b=lnb_ref[...], ones=ones_ref, w_out=wout_ref,
        g_post=gpost_ref[...], dmat=dmat, masks=masks, eye2=r2 == c2)

    n_sub = tile // sub
    shared = {-1: dict(prev_rw=prev_rw_ref[...], prev_lo=prev_lo_ref[...],
                       prev_cv1=prev_cv_ref[0:1, :], prev_cv2=prev_cv_ref[1:2, :],
                       state=state_ref[...])}
    blocks = [_mix_rows_steps(k, h_ref, o_ref, k * sub, sub, shared, consts) for k in range(n_sub)]
    for step in range((n_sub - 1) * MIX_LAG + MIX_PHASES * MIX_STEPS):
        for k, blk in enumerate(blocks):
            if 0 <= step - k * MIX_LAG < MIX_PHASES * MIX_STEPS:
                next(blk)
    final = shared[n_sub - 1]
    prev_rw_ref[...] = final["prev_rw"]
    prev_lo_ref[...] = final["prev_lo"]
    prev_cv_ref[0:1, :] = final["prev_cv1"]
    prev_cv_ref[1:2, :] = final["prev_cv2"]
    state_ref[...] = final["state"]


def _mixer(h, ada, g_pre, w_main, w_in, conv_w, mu_shift, w0, w_up, a0, a_up, g_up, k_k, k_a, r_k,
           ln_x_w, ln_x_b, w_out, g_post, casts=()):
    t = h.shape[0]
    tile = min(MIX_TILE, t)
    n_main = MIX_MAIN_COLS
    w_lo = jnp.pad(w_in[:, n_main:], ((0, 0), (0, LORA_PAD - LORA_COLS))).astype(BF16)
    mu_main = mu_shift[:3 * RWKV_WIDTH].reshape(1, -1)
    mu_lo = jnp.pad(mu_shift[3 * RWKV_WIDTH:], (0, LORA_PAD - LORA_COLS)).reshape(1, -1)
    lora_up = jnp.zeros((LORA_PAD, 3 * RWKV_WIDTH), F32)
    lora_up = lora_up.at[:DECAY_LORA, :RWKV_WIDTH].set(w_up)
    lora_up = lora_up.at[DECAY_LORA:DECAY_LORA + AAA_LORA, RWKV_WIDTH:2 * RWKV_WIDTH].set(a_up)
    lora_up = lora_up.at[DECAY_LORA + AAA_LORA:LORA_COLS, 2 * RWKV_WIDTH:].set(g_up)
    head_id = jnp.arange(MXU_DIM) // RWKV_HEAD
    ones_bd = (head_id[:, None] == head_id[None, :]).astype(BF16)
    row = lambda z: z.reshape(1, -1)
    const = lambda i: (0, 0)
    full = lambda shape: pl.BlockSpec(shape, const)
    cast_in, cast_out, cast_shapes = _cast_plan(casts, t // tile)
    out = pl.pallas_call(
        functools.partial(_mixer_kernel, len(casts)),
        grid=(t // tile,),
        in_specs=[
            pl.BlockSpec((tile, D_MODEL), lambda i: (i, 0)),
            pl.BlockSpec((N_ADA, 1, D_MODEL), lambda i: (0, 0, 0)),
            full((1, D_MODEL)),
            full((D_MODEL, n_main)),
            full((D_MODEL, LORA_PAD)),
            full((1, 3 * RWKV_WIDTH)),
            full((1, LORA_PAD)),
            full((CONV_K, CONV_WIDTH)),
            full((1, RWKV_WIDTH)),
            full((1, RWKV_WIDTH)),
            full((LORA_PAD, 3 * RWKV_WIDTH)),
            full((1, RWKV_WIDTH)),
            full((1, RWKV_WIDTH)),
            full((1, RWKV_WIDTH)),
            full((1, RWKV_WIDTH)),
            full((1, RWKV_WIDTH)),
            full((MXU_DIM, MXU_DIM)),
            full((D_MODEL, D_MODEL)),
            full((1, D_MODEL)),
        ] + cast_in,
        out_specs=[pl.BlockSpec((tile, D_MODEL), lambda i: (i, 0))] + cast_out,
        out_shape=[jax.ShapeDtypeStruct((t, D_MODEL), F32)] + cast_shapes,
        scratch_shapes=[
            pltpu.VMEM((1, 3 * RWKV_WIDTH), F32),
            pltpu.VMEM((1, LORA_PAD), F32),
            pltpu.VMEM((2, CONV_WIDTH), F32),
            pltpu.VMEM((N_PAIRS, LANES, LANES), F32),
        ],
        compiler_params=pltpu.CompilerParams(dimension_semantics=("arbitrary",),
                                             vmem_limit_bytes=VMEM_LIMIT),
        name="token_mixing",
    )(h, ada, row(g_pre), w_main, w_lo, mu_main, mu_lo, conv_w, row(w0), row(a0),
      lora_up.astype(BF16), row(k_k), row(k_a), row(r_k), row(ln_x_w), row(ln_x_b), ones_bd,
      w_out, row(g_post), *[arr for arr, _ in casts])
    return out[0], out[1:]


def kernel(x, c, w_ada, b_ada, ffn1_g_pre, ffn1_w1, ffn1_w3, ffn1_w2, ffn1_g_post, mix_g_pre, w_in, conv_w, mu_shift, w0, w_up, a0, a_up, g_up, k_k, k_a, r_k, ln_x_w, ln_x_b, w_out, mix_g_post, ffn2_g_pre, ffn2_w1, ffn2_w3, ffn2_w2, ffn2_g_post):
    bsz, t, _ = x.shape
    outs = []
    for b in range(bsz):
        h = x[b]
        for l in range(w_ada.shape[0]):
            ada = _ada(c[b], w_ada[l], b_ada[l])
            h, (w_main, w_out_b) = _ffn(
                h, ada, 0, ffn1_g_pre[l], ffn1_w1[l], ffn1_w3[l], ffn1_w2[l], ffn1_g_post[l],
                casts=[(w_in[l], MIX_MAIN_COLS), (w_out[l], D_MODEL)])
            h, (w1_b, w3_b, w2_b) = _mixer(
                h, ada, mix_g_pre[l], w_main, w_in[l], conv_w[l], mu_shift[l], w0[l], w_up[l],
                a0[l], a_up[l], g_up[l], k_k[l], k_a[l], r_k[l], ln_x_w[l], ln_x_b[l],
                w_out_b, mix_g_post[l],
                casts=[(ffn2_w1[l], D_FF), (ffn2_w3[l], D_FF), (ffn2_w2[l], D_MODEL)])
            h, _ = _ffn(h, ada, 6, ffn2_g_pre[l], w1_b, w3_b, w2_b, ffn2_g_post[l])
        outs.append(h)
    return jnp.stack(outs, axis=0)
```

```python
import functools

import jax
import jax.numpy as jnp
from jax import lax
from jax.experimental import pallas as pl
from jax.experimental.pallas import tpu as pltpu

F32 = jnp.float32
BF16 = jnp.bfloat16

D_MODEL = 1024
D_FF = 2816
CONV_WIDTH = 512
CONV_K = 3
RWKV_WIDTH = 512
RWKV_HEAD = 64
RWKV_HEADS = RWKV_WIDTH // RWKV_HEAD
DECAY_LORA = 32
AAA_LORA = 32
GATE_LORA = 96
LORA_COLS = DECAY_LORA + AAA_LORA + GATE_LORA
N_ADA = 9
MACARON_W = 0.5
NORM_EPS = 1e-6
GN_EPS = 64e-5

LANES = 128
SUBLANES = 8
MXU_DIM = 256
LORA_PAD = MXU_DIM
BF16_ROWS = 16
MIX_MAIN_COLS = 3 * CONV_WIDTH + 3 * RWKV_WIDTH
MIX_INPUTS = 19
HEADS_PER_PAIR = LANES // RWKV_HEAD
N_PAIRS = RWKV_WIDTH // LANES
CHUNK = 64
SUB = 8
MIX_TILE = 512
MIX_SUBTILE = 256
MIX_PHASES = 5
MIX_LAG = 8
MIX_STEPS = 8
FFN_TILE = 1024
FFN_SUBTILE = 256
FFN_LAG = 4
FFN_COLS = MXU_DIM
ADA_ROWS = 128
VMEM_LIMIT = 56 * 1024 * 1024


def _rms(x):
    return x * lax.rsqrt(jnp.mean(x * x, axis=-1, keepdims=True) + NORM_EPS)


def _dot(a, b):
    return jnp.dot(a.astype(BF16), b.astype(BF16), preferred_element_type=F32)


def _ada_kernel(c_ref, w_ref, b_ref, o_ref):
    @pl.when(pl.program_id(0) == 0)
    def _():
        o_ref[...] = b_ref[...]

    c = c_ref[...]
    cond = c * jax.nn.sigmoid(c)
    part = jnp.sum(w_ref[...] * cond, axis=0, keepdims=True)
    for k in range(N_ADA):
        o_ref[k] += part[:, k * D_MODEL:(k + 1) * D_MODEL]


def _ada(c, w_ada, b_ada):
    full = pl.BlockSpec((N_ADA, 1, D_MODEL), lambda j: (0, 0, 0))
    return pl.pallas_call(
        _ada_kernel,
        grid=(D_MODEL // ADA_ROWS,),
        in_specs=[
            pl.BlockSpec((ADA_ROWS, 1), lambda j: (j, 0)),
            pl.BlockSpec((ADA_ROWS, N_ADA * D_MODEL), lambda j: (j, 0)),
            full,
        ],
        out_specs=full,
        out_shape=jax.ShapeDtypeStruct((N_ADA, 1, D_MODEL), F32),
        compiler_params=pltpu.CompilerParams(dimension_semantics=("arbitrary",)),
        name="ada_proj",
    )(c.reshape(D_MODEL, 1), w_ada, b_ada.reshape(N_ADA, 1, D_MODEL))


def _ffn_rows_steps(h_ref, o_ref, act_ref, row0, rows, c):
    x = h_ref[row0:row0 + rows, :]
    u = (_rms(x) * c["g_pre"] * (1.0 + c["scale"]) + c["shift"]).astype(BF16)
    yield
    for j in range(D_FF // FFN_COLS):
        cols = slice(j * FFN_COLS, (j + 1) * FFN_COLS)
        a = jnp.dot(u, c["w1"][:, cols], preferred_element_type=F32)
        b = jnp.dot(u, c["w3"][:, cols], preferred_element_type=F32)
        act_ref[row0:row0 + rows, cols] = (a * jax.nn.sigmoid(a) * b).astype(BF16)
        yield
    act = act_ref[row0:row0 + rows, :]
    y_cols = []
    for j in range(D_MODEL // FFN_COLS):
        cols = slice(j * FFN_COLS, (j + 1) * FFN_COLS)
        y_cols.append(jnp.dot(act, c["w2"][:, cols], preferred_element_type=F32))
        yield
    y = jnp.concatenate(y_cols, axis=1)
    o_ref[row0:row0 + rows, :] = x + MACARON_W * c["gate"] * (_rms(y) * c["g_post"])
    yield


def _cast_plan(casts, n_steps):
    in_specs, out_specs, out_shapes = [], [], []
    for arr, cols in casts:
        rows = arr.shape[0]
        n_blk = n_steps
        while rows % n_blk or (rows // n_blk) % BF16_ROWS:
            n_blk //= 2
        idx = lambda i, every=n_steps // n_blk: (i // every, 0)
        in_specs.append(pl.BlockSpec((rows // n_blk, arr.shape[1]), idx))
        out_specs.append(pl.BlockSpec((rows // n_blk, cols), idx))
        out_shapes.append(jax.ShapeDtypeStruct((rows, cols), BF16))
    return in_specs, out_specs, out_shapes


def _cast_blocks(src_refs, dst_refs):
    for src, dst in zip(src_refs, dst_refs):
        dst[...] = src[:, :dst.shape[1]].astype(BF16)


def _ffn_kernel(ada_row, n_cast, *refs):
    h_ref, ada_ref, gpre_ref, w1_ref, w3_ref, w2_ref, gpost_ref = refs[:7]
    o_ref, act_ref = refs[7 + n_cast], refs[-1]
    _cast_blocks(refs[7:7 + n_cast], refs[8 + n_cast:8 + 2 * n_cast])
    tile = h_ref.shape[0]
    sub = min(FFN_SUBTILE, tile)
    consts = dict(shift=ada_ref[ada_row], scale=ada_ref[ada_row + 1], gate=ada_ref[ada_row + 2],
                  g_pre=gpre_ref[...], g_post=gpost_ref[...], w1=w1_ref, w3=w3_ref, w2=w2_ref)
    blocks = [_ffn_rows_steps(h_ref, o_ref, act_ref, r0, sub, consts) for r0 in range(0, tile, sub)]
    n_steps = 2 + D_FF // FFN_COLS + D_MODEL // FFN_COLS
    for step in range((len(blocks) - 1) * FFN_LAG + n_steps):
        for k, blk in enumerate(blocks):
            if 0 <= step - k * FFN_LAG < n_steps:
                next(blk)


def _ffn(h, ada, ada_row, g_pre, w1, w3, w2, g_post, casts=()):
    t = h.shape[0]
    tile = min(FFN_TILE, t)
    const = lambda i: (0, 0)
    cast_in, cast_out, cast_shapes = _cast_plan(casts, t // tile)
    out = pl.pallas_call(
        functools.partial(_ffn_kernel, ada_row, len(casts)),
        grid=(t // tile,),
        in_specs=[
            pl.BlockSpec((tile, D_MODEL), lambda i: (i, 0)),
            pl.BlockSpec((N_ADA, 1, D_MODEL), lambda i: (0, 0, 0)),
            pl.BlockSpec((1, D_MODEL), const),
            pl.BlockSpec((D_MODEL, D_FF), const, pipeline_mode=pl.Buffered(1)),
            pl.BlockSpec((D_MODEL, D_FF), const, pipeline_mode=pl.Buffered(1)),
            pl.BlockSpec((D_FF, D_MODEL), const, pipeline_mode=pl.Buffered(1)),
            pl.BlockSpec((1, D_MODEL), const),
        ] + cast_in,
        out_specs=[pl.BlockSpec((tile, D_MODEL), lambda i: (i, 0))] + cast_out,
        out_shape=[jax.ShapeDtypeStruct((t, D_MODEL), F32)] + cast_shapes,
        scratch_shapes=[pltpu.VMEM((tile, D_FF), BF16)],
        compiler_params=pltpu.CompilerParams(dimension_semantics=("arbitrary",),
                                             vmem_limit_bytes=VMEM_LIMIT),
        name="swiglu_half_step",
    )(h, ada, g_pre.reshape(1, D_MODEL), w1.astype(BF16), w3.astype(BF16), w2.astype(BF16),
      g_post.reshape(1, D_MODEL), *[arr for arr, _ in casts])
    return out[0], out[1:]


def _shift_rows(x, fill_row):
    row = lax.broadcasted_iota(jnp.int32, x.shape, 0)
    return jnp.where(row == 0, fill_row, pltpu.roll(x, 1, 0))


def _softplus(z):
    return jnp.maximum(z, 0.0) + jnp.log(1.0 + jnp.exp(-jnp.abs(z)))


def _bmm(a, b):
    return jnp.einsum("bmk,bkn->bmn", a.astype(BF16), b.astype(BF16), preferred_element_type=F32)


def _bmm_nt(a, b):
    return jnp.einsum("bmk,bnk->bmn", a.astype(BF16), b.astype(BF16), preferred_element_type=F32)


def _bmm_tn(a, b):
    return jnp.einsum("bkm,bkn->bmn", a.astype(BF16), b.astype(BF16), preferred_element_type=F32)


def _block_diag(x, left):
    x = x.astype(BF16)
    zero = jnp.zeros_like(x)
    return jnp.concatenate([jnp.where(left, x, zero), jnp.where(left, zero, x)], axis=1)


def _chunk_terms_steps(ph, rh, qh, kh, pt, rt, qb, kb, v, masks, res):
    left, strict, incl, diag_blk, eye, bd_mask = masks
    bd = lambda x: _block_diag(x, left)
    mm = lambda a, b: _bmm(a, bd(b))
    gram = _bmm_nt(jnp.concatenate([ph, rh], axis=1),
                   jnp.concatenate([bd(qh), bd(kh)], axis=1))
    a_pq = jnp.where(strict, gram[:, :CHUNK, :LANES], 0.0)
    a_rq = jnp.where(incl, gram[:, CHUNK:, :LANES], 0.0)
    a_pk = jnp.where(strict, gram[:, :CHUNK, LANES:], 0.0)
    a_rk = jnp.where(incl, gram[:, CHUNK:, LANES:], 0.0)
    yield
    a_d = jnp.where(diag_blk, a_pq, 0.0)
    a_o = a_pq - a_d
    x2 = mm(a_d, a_d)
    t_d = eye + a_d
    yield
    both = mm(jnp.concatenate([x2, t_d], axis=1), x2)
    x4, t_d = both[:, :CHUNK], t_d + both[:, CHUNK:]
    yield
    t_d = t_d + mm(t_d, x4)
    n1 = mm(t_d, a_o)
    yield
    n2 = mm(n1, n1)
    s = eye + n1
    yield
    both = mm(jnp.concatenate([n2, s], axis=1), n2)
    n4, s = both[:, :CHUNK], s + both[:, CHUNK:]
    yield
    s = s + mm(s, n4)
    t_inv = mm(s, t_d)
    both = mm(jnp.concatenate([a_pk, a_rk], axis=1), v)
    av, ark_v = both[:, :CHUNK], both[:, CHUNK:]
    yield
    wu = _bmm(t_inv, jnp.concatenate([bd(pt), bd(av)], axis=2))
    w, ub = wu[:, :, :LANES], wu[:, :, LANES:]
    ru = _bmm(a_rq, jnp.concatenate([bd(w), bd(ub)], axis=2))
    res["rw"] = rt + ru[:, :, :LANES]
    res["y0"] = ru[:, :, LANES:] + ark_v
    res["g_off"] = jnp.where(bd_mask, _bmm_tn(qb, w), 0.0)
    res["c_bd"] = jnp.where(bd_mask, _bmm_tn(jnp.concatenate([qb, kb], axis=1),
                                             jnp.concatenate([ub, v], axis=1)), 0.0)
    yield


def _seg_sum(z, ones_bd):
    half = ones_bd.shape[0]
    return jnp.concatenate([_dot(z[:, :half], ones_bd), _dot(z[:, half:], ones_bd)], axis=1)


def _mix_rows_steps(k, h_ref, o_ref, row0, rows, shared, c):
    n_chunks = rows // CHUNK
    cw, rwd = CONV_WIDTH, RWKV_WIDTH
    x = h_ref[row0:row0 + rows, :]
    u = (_rms(x) * c["g_pre"] * (1.0 + c["scale"]) + c["shift"]).astype(BF16)
    yield
    p_cols = []
    for j in range(6):
        p_cols.append(jnp.dot(u, c["w_in"][:, j * cw:(j + 1) * cw], preferred_element_type=F32))
        yield
    plo = jnp.dot(u, c["w_lo"][...], preferred_element_type=F32)
    c_pre, c_post, c_val = p_cols[0], p_cols[1], p_cols[2]
    rw_raw = jnp.concatenate(p_cols[3:], axis=1)
    cv = c_pre * c_val
    shared[k] = dict(prev_rw=rw_raw[rows - 1:, :], prev_lo=plo[rows - 1:, :],
                     prev_cv1=cv[rows - 1:, :], prev_cv2=cv[rows - 2:rows - 1, :])
    yield

    prev = shared[k - 1]
    conv_w = c["conv_w"]
    cv1 = _shift_rows(cv, prev["prev_cv1"])
    cv2 = _shift_rows(cv1, prev["prev_cv2"])
    y_conv = c_post * (conv_w[0:1, :] * cv2 + conv_w[1:2, :] * cv1 + conv_w[2:3, :] * cv)
    yield
    rw_mix = rw_raw + (_shift_rows(rw_raw, prev["prev_rw"]) - rw_raw) * c["mu"]
    lo_mix = plo + (_shift_rows(plo, prev["prev_lo"]) - plo) * c["mu_lo"]
    xr, xk, xv = rw_mix[:, :rwd], rw_mix[:, rwd:2 * rwd], rw_mix[:, 2 * rwd:]
    yield
    lane_lo = lax.broadcasted_iota(jnp.int32, lo_mix.shape, 1)
    lo_act = jnp.where(lane_lo < DECAY_LORA, jnp.tanh(lo_mix),
                       jnp.where(lane_lo < DECAY_LORA + AAA_LORA, lo_mix, jax.nn.sigmoid(lo_mix)))
    lup = _dot(lo_act, c["lora_up"][...])
    yield
    w_raw = -_softplus(-(c["w0"] + lup[:, :rwd])) - 0.5
    logw = -jnp.exp(w_raw)
    a = jax.nn.sigmoid(c["a0"] + lup[:, rwd:2 * rwd])
    g = lup[:, 2 * rwd:]
    yield
    ones_bd = c["ones"][...]
    kk = xk * c["k_k"]
    kk = kk / jnp.maximum(jnp.sqrt(_seg_sum(kk * kk, ones_bd)), 1e-12)
    kmod = xk * (1.0 + (a - 1.0) * c["k_a"])
    bonus = _seg_sum(xr * kmod * c["r_k"], ones_bd) * xv
    pv = -kk
    qv = kk * a
    yield
    logw_hi = logw.astype(BF16)
    logw_lo = (logw - logw_hi.astype(F32)).astype(BF16)
    gc = (jnp.dot(c["dmat"], logw_hi, preferred_element_type=F32)
          + jnp.dot(c["dmat"], logw_lo, preferred_element_type=F32))
    e_fwd = jnp.exp(gc)
    e_bwd = jnp.exp(-gc)
    e_prev = jnp.exp(gc - logw)
    yield
    per_chunk = lambda z: z.reshape(n_chunks, CHUNK, rwd)
    gc3, lw3 = per_chunk(gc), per_chunk(logw)
    g_mid = lw3[:, 0:1, :] - gc3[:, 0:1, :]
    g_end = gc3[:, CHUNK - 1:CHUNK, :]
    e_mid = jnp.exp(g_mid)
    e_end = jnp.exp(g_end)
    decay_end = jnp.exp(g_mid + g_end)
    scale_chunk = lambda z, f: (per_chunk(z) * f).reshape(rows, rwd)
    ph = pv * e_prev
    rh = xr * e_fwd
    qh = qv * e_bwd
    kh = kmod * e_bwd
    yield
    pt = scale_chunk(ph, e_mid)
    rt = scale_chunk(rh, e_mid)
    qb = scale_chunk(qh, e_end)
    kb = scale_chunk(kh, e_end)

    def batch(z):
        return jnp.stack([z[ci * CHUNK:(ci + 1) * CHUNK, j * LANES:(j + 1) * LANES]
                          for ci in range(n_chunks) for j in range(N_PAIRS)], axis=0)

    terms = {}
    chunk_steps = _chunk_terms_steps(batch(ph), batch(rh), batch(qh), batch(kh), batch(pt),
                                     batch(rt), batch(qb), batch(kb), batch(xv), c["masks"], terms)
    yield

    for _ in chunk_steps:
        yield

    m = shared[k - 1]["state"]
    o_rows = []
    for ci in range(n_chunks):
        b = slice(ci * N_PAIRS, (ci + 1) * N_PAIRS)
        o_c = _bmm(terms["rw"][b], m) + terms["y0"][b]
        o_rows.append(jnp.concatenate([o_c[j] for j in range(N_PAIRS)], axis=1))
        dec = jnp.stack([jnp.where(c["eye2"], decay_end[ci, :, j * LANES:(j + 1) * LANES], 0.0)
                         for j in range(N_PAIRS)], axis=0)
        m = jnp.sum(dec, axis=2, keepdims=True) * m + _bmm(terms["g_off"][b], m) + terms["c_bd"][b]
        if ci == n_chunks - 1:
            shared[k]["state"] = m
        yield
    o = jnp.concatenate(o_rows, axis=0)
    inv_n = 1.0 / RWKV_HEAD
    o_c = o - _seg_sum(o, ones_bd) * inv_n
    yield
    o_var = _seg_sum(o_c * o_c, ones_bd) * inv_n
    o_n = o_c * lax.rsqrt(o_var + GN_EPS) * c["ln_w"] + c["ln_b"]
    y_rwkv = (o_n + bonus) * g
    y_in = jnp.concatenate([y_conv, y_rwkv], axis=1).astype(BF16)
    yield
    yield
    yield

    y_cols = []
    for j in range(D_MODEL // MXU_DIM):
        y_cols.append(jnp.dot(y_in, c["w_out"][:, j * MXU_DIM:(j + 1) * MXU_DIM],
                              preferred_element_type=F32))
        yield
    y = jnp.concatenate(y_cols, axis=1)
    o_ref[row0:row0 + rows, :] = x + c["gate"] * (_rms(y) * c["g_post"])
    yield
    yield
    yield
    yield


def _mixer_kernel(n_cast, *refs):
    (h_ref, ada_ref, gpre_ref, win_ref, wlo_ref, mu_ref, mulo_ref, convw_ref, w0_ref, a0_ref,
     loraup_ref, kk_ref, ka_ref, rk_ref, lnw_ref, lnb_ref, ones_ref, wout_ref,
     gpost_ref) = refs[:MIX_INPUTS]
    o_ref = refs[MIX_INPUTS + n_cast]
    prev_rw_ref, prev_lo_ref, prev_cv_ref, state_ref = refs[-4:]
    _cast_blocks(refs[MIX_INPUTS:MIX_INPUTS + n_cast],
                 refs[MIX_INPUTS + n_cast + 1:MIX_INPUTS + 2 * n_cast + 1])
    tile = h_ref.shape[0]
    sub = min(MIX_SUBTILE, tile)
    assert sub // CHUNK == MIX_STEPS // 2

    @pl.when(pl.program_id(0) == 0)
    def _():
        prev_rw_ref[...] = jnp.zeros_like(prev_rw_ref)
        prev_lo_ref[...] = jnp.zeros_like(prev_lo_ref)
        prev_cv_ref[...] = jnp.zeros_like(prev_cv_ref)
        state_ref[...] = jnp.zeros_like(state_ref)

    ti = lax.broadcasted_iota(jnp.int32, (sub, sub), 0)
    si = lax.broadcasted_iota(jnp.int32, (sub, sub), 1)
    tl, sl = ti % CHUNK, si % CHUNK
    mid = CHUNK // 2 - 1
    dmat = jnp.where((ti // CHUNK) == (si // CHUNK),
                     (sl <= tl).astype(F32) - (sl <= mid).astype(F32), 0.0).astype(BF16)
    r_i = lax.broadcasted_iota(jnp.int32, (CHUNK, LANES), 0)
    l_i = lax.broadcasted_iota(jnp.int32, (CHUNK, LANES), 1)
    col = l_i % RWKV_HEAD
    r2 = lax.broadcasted_iota(jnp.int32, (LANES, LANES), 0)
    c2 = lax.broadcasted_iota(jnp.int32, (LANES, LANES), 1)
    masks = (l_i < RWKV_HEAD, r_i > col, r_i >= col, (r_i // SUB) == (col // SUB),
             (r_i == col).astype(F32), (r2 // RWKV_HEAD) == (c2 // RWKV_HEAD))
    consts = dict(
        shift=ada_ref[3], scale=ada_ref[4], gate=ada_ref[5], g_pre=gpre_ref[...],
        w_in=win_ref, w_lo=wlo_ref, mu=mu_ref[...], mu_lo=mulo_ref[...], conv_w=convw_ref[...],
        w0=w0_ref[...], a0=a0_ref[...], lora_up=loraup_ref, k_k=kk_ref[...], k_a=ka_ref[...],
        r_k=rk_ref[...], ln_w=lnw_ref[...], ln_b=lnb_ref[...], ones=ones_ref, w_out=wout_ref,
        g_post=gpost_ref[...], dmat=dmat, masks=masks, eye2=r2 == c2)

    n_sub = tile // sub
    shared = {-1: dict(prev_rw=prev_rw_ref[...], prev_lo=prev_lo_ref[:, :LORA_PAD],
                       prev_cv1=prev_cv_ref[0:1, :], prev_cv2=prev_cv_ref[1:2, :],
                       state=state_ref[...])}
    blocks = [_mix_rows_steps(k, h_ref, o_ref, k * sub, sub, shared, consts) for k in range(n_sub)]
    for step in range((n_sub - 1) * MIX_LAG + MIX_PHASES * MIX_STEPS):
        for k, blk in enumerate(blocks):
            if 0 <= step - k * MIX_LAG < MIX_PHASES * MIX_STEPS:
                next(blk)
    final = shared[n_sub - 1]
    prev_rw_ref[...] = final["prev_rw"]
    prev_lo_ref[:, :LORA_PAD] = final["prev_lo"]
    prev_cv_ref[0:1, :] = final["prev_cv1"]
    prev_cv_ref[1:2, :] = final["prev_cv2"]
    state_ref[...] = final["state"]


def _mixer(h, ada, g_pre, w_main, w_in, conv_w, mu_shift, w0, w_up, a0, a_up, g_up, k_k, k_a, r_k,
           ln_x_w, ln_x_b, w_out, g_post, casts=()):
    t = h.shape[0]
    tile = min(MIX_TILE, t)
    n_main = MIX_MAIN_COLS
    w_lo = jnp.pad(w_in[:, n_main:], ((0, 0), (0, LORA_PAD - LORA_COLS))).astype(BF16)
    mu_main = mu_shift[:3 * RWKV_WIDTH].reshape(1, -1)
    mu_lo = jnp.pad(mu_shift[3 * RWKV_WIDTH:], (0, LORA_PAD - LORA_COLS)).reshape(1, -1)
    lora_up = jnp.zeros((LORA_PAD, 3 * RWKV_WIDTH), F32)
    lora_up = lora_up.at[:DECAY_LORA, :RWKV_WIDTH].set(w_up)
    lora_up = lora_up.at[DECAY_LORA:DECAY_LORA + AAA_LORA, RWKV_WIDTH:2 * RWKV_WIDTH].set(a_up)
    lora_up = lora_up.at[DECAY_LORA + AAA_LORA:LORA_COLS, 2 * RWKV_WIDTH:].set(g_up)
    head_id = jnp.arange(MXU_DIM) // RWKV_HEAD
    ones_bd = (head_id[:, None] == head_id[None, :]).astype(BF16)
    row = lambda z: z.reshape(1, -1)
    const = lambda i: (0, 0)
    full = lambda shape: pl.BlockSpec(shape, const)
    cast_in, cast_out, cast_shapes = _cast_plan(casts, t // tile)
    out = pl.pallas_call(
        functools.partial(_mixer_kernel, len(casts)),
        grid=(t // tile,),
        in_specs=[
            pl.BlockSpec((tile, D_MODEL), lambda i: (i, 0)),
            pl.BlockSpec((N_ADA, 1, D_MODEL), lambda i: (0, 0, 0)),
            full((1, D_MODEL)),
            full((D_MODEL, n_main)),
            full((D_MODEL, LORA_PAD)),
            full((1, 3 * RWKV_WIDTH)),
            full((1, LORA_PAD)),
            full((CONV_K, CONV_WIDTH)),
            full((1, RWKV_WIDTH)),
            full((1, RWKV_WIDTH)),
            full((LORA_PAD, 3 * RWKV_WIDTH)),
            full((1, RWKV_WIDTH)),
            full((1, RWKV_WIDTH)),
            full((1, RWKV_WIDTH)),
            full((1, RWKV_WIDTH)),
            full((1, RWKV_WIDTH)),
            full((MXU_DIM, MXU_DIM)),
            full((D_MODEL, D_MODEL)),
            full((1, D_MODEL)),
        ] + cast_in,
        out_specs=[pl.BlockSpec((tile, D_MODEL), lambda i: (i, 0))] + cast_out,
        out_shape=[jax.ShapeDtypeStruct((t, D_MODEL), F32)] + cast_shapes,
        scratch_shapes=[
            pltpu.VMEM((1, 3 * RWKV_WIDTH), F32),
            pltpu.VMEM((1, 2 * LORA_PAD), F32),
            pltpu.VMEM((2, CONV_WIDTH), F32),
            pltpu.VMEM((N_PAIRS, LANES, LANES), F32),
        ],
        compiler_params=pltpu.CompilerParams(dimension_semantics=("arbitrary",),
                                             vmem_limit_bytes=VMEM_LIMIT),
        name="token_mixing",
    )(h, ada, row(g_pre), w_main, w_lo, mu_main, mu_lo, conv_w, row(w0), row(a0),
      lora_up.astype(BF16), row(k_k), row(k_a), row(r_k), row(ln_x_w), row(ln_x_b), ones_bd,
      w_out, row(g_post), *[arr for arr, _ in casts])
    return out[0], out[1:]


def kernel(x, c, w_ada, b_ada, ffn1_g_pre, ffn1_w1, ffn1_w3, ffn1_w2, ffn1_g_post, mix_g_pre, w_in, conv_w, mu_shift, w0, w_up, a0, a_up, g_up, k_k, k_a, r_k, ln_x_w, ln_x_b, w_out, mix_g_post, ffn2_g_pre, ffn2_w1, ffn2_w3, ffn2_w2, ffn2_g_post):
    bsz, t, _ = x.shape
    outs = []
    for b in range(bsz):
        h = x[b]
        for l in range(w_ada.shape[0]):
            ada = _ada(c[b], w_ada[l], b_ada[l])
            h, (w_main, w_out_b) = _ffn(
                h, ada, 0, ffn1_g_pre[l], ffn1_w1[l], ffn1_w3[l], ffn1_w2[l], ffn1_g_post[l],
                casts=[(w_in[l], MIX_MAIN_COLS), (w_out[l], D_MODEL)])
            h, (w1_b, w3_b, w2_b) = _mixer(
                h, ada, mix_g_pre[l], w_main, w_in[l], conv_w[l], mu_shift[l], w0[l], w_up[l],
                a0[l], a_up[l], g_up[l], k_k[l], k_a[l], r_k[l], ln_x_w[l], ln_x_b[l],
                w_out_b, mix_g_post[l],
                casts=[(ffn2_w1[l], D_FF), (ffn2_w3[l], D_FF), (ffn2_w2[l], D_MODEL)])
            h, _ = _ffn(h, ada, 6, ffn2_g_pre[l], w1_b, w3_b, w2_b, ffn2_g_post[l])
        outs.append(h)
    return jnp.stack(outs, axis=0)
```
